```python
import math
import jax
import jax.numpy as jnp
from jax import lax

D_MODEL = 1024
BATCH = 16
SEQ = 256
DEPTH = 4
DEC_BATCH = 8
DEC_SEQ = 4096
PAST_LEN = 256

GRID_W = 64
MIX_WIDTH = D_MODEL
CONV_CH = D_MODEL // 4
CONV_K = 31
GDN_HEADS = 4
GDN_DK = D_MODEL // 8
GDN_DV = D_MODEL // 8
SHORT_K = 3
CHUNK = 64
NA_HEADS = 4
NA_DIM = D_MODEL // 16
WIN_H = 8
WIN_W = 16
D_FF = 11 * D_MODEL // 4
FFN_K = 3
EPS = 1e-6
PROJ_SIZES = (CONV_CH, CONV_CH,
              GDN_HEADS * GDN_DK, GDN_HEADS * GDN_DK, GDN_HEADS * GDN_DV, GDN_HEADS * GDN_DV,
              2 * GDN_HEADS, 2 * GDN_HEADS,
              NA_HEADS * NA_DIM, NA_HEADS * NA_DIM, NA_HEADS * NA_DIM)
PROJ_DIM = sum(PROJ_SIZES)

kernel_name = 'hybrid_dit_conformer_gdn_natten_step'


def rmsnorm(x, g):
    xf = x.astype(jnp.float32)
    y = xf * lax.rsqrt(jnp.mean(xf * xf, axis=-1, keepdims=True) + EPS)
    return (y * g.astype(jnp.float32)).astype(x.dtype)


def layernorm(x, g, b):
    xf = x.astype(jnp.float32)
    mu = jnp.mean(xf, axis=-1, keepdims=True)
    var = jnp.mean(jnp.square(xf - mu), axis=-1, keepdims=True)
    y = (xf - mu) * lax.rsqrt(var + EPS)
    return (y * g.astype(jnp.float32) + b.astype(jnp.float32)).astype(x.dtype)


def l2norm(x):
    return x * lax.rsqrt(jnp.sum(x * x, axis=-1, keepdims=True) + EPS)


def dwconv(x, w):
    k = w.shape[0]
    return lax.conv_general_dilated(
        x, w[:, None, :].astype(x.dtype), window_strides=(1,), padding=[(k // 2, k // 2)],
        dimension_numbers=('NWC', 'WIO', 'NWC'), feature_group_count=x.shape[-1])


def split_projection(p):
    offsets, acc = [], 0
    for s in PROJ_SIZES[:-1]:
        acc += s
        offsets.append(acc)
    return jnp.split(p, offsets, axis=-1)


def conformer_conv(a_val, a_gate, w_dw, b_dw, ln_g, ln_b):
    u = a_val * jax.nn.sigmoid(a_gate)
    u = dwconv(u, w_dw) + b_dw
    return jax.nn.silu(layernorm(u, ln_g, ln_b))


def chunk_gated_delta(q, k, v, g, beta, s0):
    b, h, seq, dk = q.shape
    dv = v.shape[-1]
    n = seq // CHUNK
    q = q.reshape(b, h, n, CHUNK, dk)
    k = k.reshape(b, h, n, CHUNK, dk)
    v = v.reshape(b, h, n, CHUNK, dv)
    g = g.reshape(b, h, n, CHUNK)
    beta = beta.reshape(b, h, n, CHUNK)
    gc = jnp.cumsum(g, axis=-1)
    idx = jnp.arange(CHUNK)
    tril = idx[:, None] >= idx[None, :]
    strict = idx[:, None] > idx[None, :]
    decay = jnp.exp(jnp.where(tril, gc[..., :, None] - gc[..., None, :], -jnp.inf))
    kb = k * beta[..., None]
    lmat = jnp.where(strict, jnp.einsum('bhnid,bhnjd->bhnij', kb, k) * decay, 0.0)
    rhs = jnp.concatenate([v * beta[..., None], kb * jnp.exp(gc)[..., None]], axis=-1)
    sol = lax.linalg.triangular_solve(jnp.eye(CHUNK, dtype=q.dtype) + lmat, rhs,
                                      left_side=True, lower=True, unit_diagonal=True)
    u, w = sol[..., :dv], sol[..., dv:]
    intra = jnp.where(tril, jnp.einsum('bhnid,bhnjd->bhnij', q, k) * decay, 0.0)
    q_dec = q * jnp.exp(gc)[..., None]
    k_dec = k * jnp.exp(gc[..., -1:] - gc)[..., None]
    g_last = jnp.exp(gc[..., -1])

    def step(state, xs):
        qd, kd, ui, wi, it, gl = xs
        v_new = ui - jnp.einsum('bhcd,bhde->bhce', wi, state)
        o = jnp.einsum('bhcd,bhde->bhce', qd, state) + jnp.einsum('bhij,bhje->bhie', it, v_new)
        state = state * gl[..., None, None] + jnp.einsum('bhcd,bhce->bhde', kd, v_new)
        return state, o

    xs = tuple(jnp.moveaxis(t, 2, 0) for t in (q_dec, k_dec, u, w, intra, g_last))
    s_final, o = lax.scan(step, s0, xs)
    o = jnp.moveaxis(o, 0, 2).reshape(b, h, seq, dv)
    return o, s_final


def gated_deltanet(q, k, v, z, beta_raw, decay_raw, w_sc, a_log, dt_bias, norm_w, s0_f, s0_b):
    b, seq, _ = q.shape
    f32 = jnp.float32
    qkv = jax.nn.silu(dwconv(jnp.concatenate([q, k, v], axis=-1), w_sc))
    q, k, v = jnp.split(qkv, [GDN_HEADS * GDN_DK, 2 * GDN_HEADS * GDN_DK], axis=-1)

    def heads(t, d):
        return t.reshape(b, seq, GDN_HEADS, d).transpose(0, 2, 1, 3).astype(f32)

    qh = l2norm(heads(q, GDN_DK)) * (GDN_DK ** -0.5)
    kh = l2norm(heads(k, GDN_DK))
    vh = heads(v, GDN_DV)
    beta = jax.nn.sigmoid(beta_raw.astype(f32)).reshape(b, seq, 2, GDN_HEADS).transpose(2, 0, 3, 1)
    a_in = decay_raw.astype(f32).reshape(b, seq, 2, GDN_HEADS).transpose(2, 0, 3, 1)
    g = -jnp.exp(a_log.astype(f32))[:, None, :, None] * jax.nn.softplus(a_in + dt_bias.astype(f32)[:, None, :, None])
    o_f, s_f = chunk_gated_delta(qh, kh, vh, g[0], beta[0], s0_f)

    def flip(t):
        return jnp.flip(t, axis=2)

    o_b, s_b = chunk_gated_delta(flip(qh), flip(kh), flip(vh), flip(g[1]), flip(beta[1]), s0_b)
    o = (o_f + flip(o_b)).transpose(0, 2, 1, 3)
    o = rmsnorm(o, norm_w) * jax.nn.silu(z.reshape(b, seq, GDN_HEADS, GDN_DV).astype(f32))
    return o.reshape(b, seq, GDN_HEADS * GDN_DV).astype(z.dtype), s_f, s_b


def context_attention(q, k, v):
    b, seq, _ = q.shape
    qh, kh, vh = (t.reshape(b, seq, NA_HEADS, NA_DIM).transpose(0, 2, 1, 3) for t in (q, k, v))
    s = jnp.einsum('bhqd,bhkd->bhqk', qh, kh).astype(jnp.float32) * (NA_DIM ** -0.5)
    p = jax.nn.softmax(s, axis=-1).astype(v.dtype)
    o = jnp.einsum('bhqk,bhkd->bqhd', p, vh).reshape(b, seq, NA_HEADS * NA_DIM)
    return o, kh, vh


def neighbourhood_attention(q, k, v, k_ctx, v_ctx, rpb):
    b, n, _ = q.shape
    rows = n // GRID_W
    wr = min(WIN_H, rows)
    qg, kg, vg = (t.reshape(b, rows, GRID_W, NA_HEADS, NA_DIM) for t in (q, k, v))
    r = jnp.arange(rows)
    row_start = jnp.clip(r - wr // 2, 0, rows - wr)
    row_idx = row_start[:, None] + jnp.arange(wr)[None, :]
    k_win = kg[:, row_idx]
    v_win = vg[:, row_idx]
    col = jnp.arange(GRID_W)
    col_start = jnp.clip(col - WIN_W // 2, 0, GRID_W - WIN_W)
    col_ok = (col[None, :] >= col_start[:, None]) & (col[None, :] < col_start[:, None] + WIN_W)
    dr = row_idx - r[:, None] + (WIN_H - 1)
    dc = jnp.clip(col[None, :] - col[:, None] + (WIN_W - 1), 0, 2 * WIN_W - 2)
    bias = rpb[:, dr[:, None, :, None], dc[None, :, None, :]]
    scale = NA_DIM ** -0.5
    s_loc = jnp.einsum('brqhd,brikhd->bhrqik', qg, k_win).astype(jnp.float32) * scale + bias.astype(jnp.float32)
    s_loc = jnp.where(col_ok[:, None, :], s_loc, -jnp.inf).reshape(b, NA_HEADS, rows, GRID_W, wr * GRID_W)
    s_ctx = jnp.einsum('brqhd,bhkd->bhrqk', qg, k_ctx).astype(jnp.float32) * scale
    p = jax.nn.softmax(jnp.concatenate([s_loc, s_ctx], axis=-1), axis=-1).astype(v.dtype)
    p_loc = p[..., :wr * GRID_W].reshape(b, NA_HEADS, rows, GRID_W, wr, GRID_W)
    p_ctx = p[..., wr * GRID_W:]
    o = jnp.einsum('bhrqik,brikhd->brqhd', p_loc, v_win) + jnp.einsum('bhrqk,bhkd->brqhd', p_ctx, v_ctx)
    return o.reshape(b, n, NA_HEADS * NA_DIM)


def mixing(h, lp, ctx):
    (a_val, a_gate, q_b, k_b, v_b, z_b, beta_raw, decay_raw,
     q_c, k_c, v_c) = split_projection(h @ lp['w_in'])
    o_a = conformer_conv(a_val, a_gate, lp['conv_w'], lp['conv_b'], lp['conv_ln_g'], lp['conv_ln_b'])
    if ctx is None:
        s0 = jnp.zeros((h.shape[0], GDN_HEADS, GDN_DK, GDN_DV), jnp.float32)
        o_b, s_f, s_b = gated_deltanet(q_b, k_b, v_b, z_b, beta_raw, decay_raw, lp['gdn_conv_w'],
                                       lp['gdn_a_log'], lp['gdn_dt_bias'], lp['gdn_norm_w'], s0, s0)
        o_c, k_h, v_h = context_attention(q_c, k_c, v_c)
        extras = (k_h, v_h, s_f, s_b)
    else:
        k_ctx, v_ctx, s0_f, s0_b = ctx
        o_b, _, _ = gated_deltanet(q_b, k_b, v_b, z_b, beta_raw, decay_raw, lp['gdn_conv_w'],
                                   lp['gdn_a_log'], lp['gdn_dt_bias'], lp['gdn_norm_w'], s0_f, s0_b)
        o_c = neighbourhood_attention(q_c, k_c, v_c, k_ctx, v_ctx, lp['na_rpb'])
        extras = None
    out = jnp.concatenate([o_a, o_b, o_c], axis=-1) @ lp['w_out']
    return out, extras


def conv_ffn(h, w_up, w_conv, b_conv, w_down):
    u = dwconv(h @ w_up, w_conv) + b_conv
    gate, val = jnp.split(u, 2, axis=-1)
    return (jax.nn.silu(gate) * val) @ w_down


def trunk_layer(x, cvec, lp, ctx):
    sh1, sc1, g1, sh2, sc2, g2 = jnp.split(jax.nn.silu(cvec) @ lp['w_ada'] + lp['b_ada'], 6, axis=-1)
    h = rmsnorm(x, lp['g_pre_mix']) * (1.0 + sc1) + sh1
    m, extras = mixing(h, lp, ctx)
    x = x + g1 * rmsnorm(m, lp['g_post_mix'])
    h = rmsnorm(x, lp['g_pre_ffn']) * (1.0 + sc2) + sh2
    f = conv_ffn(h, lp['w_up'], lp['ffn_conv_w'], lp['ffn_conv_b'], lp['w_down'])
    x = x + g2 * rmsnorm(f, lp['g_post_ffn'])
    return x, extras


def setup_inputs(seed: int = 0) -> dict:
    key = jax.random.key(seed)
    ks = jax.random.split(key, 28)
    f32 = jnp.float32

    def nrm(k, shape, s):
        return s * jax.random.normal(k, shape, f32)

    def gain(k, shape):
        return 1.0 + 0.05 * jax.random.normal(k, shape, f32)

    dt = jnp.exp(jax.random.uniform(ks[20], (DEPTH, 2, GDN_HEADS), f32, math.log(0.001), math.log(0.1)))
    dt_bias = dt + jnp.log(-jnp.expm1(-dt))
    qkv_ch = 2 * GDN_HEADS * GDN_DK + GDN_HEADS * GDN_DV
    return {
        'x_prompt': nrm(ks[0], (BATCH, SEQ, D_MODEL), 1.0),
        'x_sample': nrm(ks[1], (DEC_BATCH, DEC_SEQ, D_MODEL), 1.0),
        'cache_attn_kv': nrm(ks[2], (DEC_BATCH, DEPTH, 2, NA_HEADS, PAST_LEN, NA_DIM), 1.0),
        'state_delta': nrm(ks[3], (DEC_BATCH, DEPTH, 2, GDN_HEADS, GDN_DK, GDN_DV), 0.1),
        'c': nrm(ks[4], (DEC_BATCH, D_MODEL), 1.0),
        'c_ctx': nrm(ks[5], (D_MODEL,), 1.0),
        'w_ada': nrm(ks[6], (DEPTH, D_MODEL, 6 * D_MODEL), 0.5 * D_MODEL ** -0.5),
        'b_ada': nrm(ks[7], (DEPTH, 6 * D_MODEL), 0.01),
        'g_pre_mix': gain(ks[8], (DEPTH, D_MODEL)),
        'g_post_mix': gain(ks[9], (DEPTH, D_MODEL)),
        'g_pre_ffn': gain(ks[10], (DEPTH, D_MODEL)),
        'g_post_ffn': gain(ks[11], (DEPTH, D_MODEL)),
        'w_in': nrm(ks[12], (DEPTH, D_MODEL, PROJ_DIM), D_MODEL ** -0.5),
        'w_out': nrm(ks[13], (DEPTH, MIX_WIDTH, D_MODEL), MIX_WIDTH ** -0.5),
        'conv_w': nrm(ks[14], (DEPTH, CONV_K, CONV_CH), CONV_K ** -0.5),
        'conv_b': nrm(ks[15], (DEPTH, CONV_CH), 0.01),
        'conv_ln_g': gain(ks[16], (DEPTH, CONV_CH)),
        'conv_ln_b': nrm(ks[17], (DEPTH, CONV_CH), 0.01),
        'gdn_conv_w': nrm(ks[18], (DEPTH, SHORT_K, qkv_ch), SHORT_K ** -0.5),
        'gdn_a_log': jnp.log(jax.random.uniform(ks[19], (DEPTH, 2, GDN_HEADS), f32, 1.0, 16.0)),
        'gdn_dt_bias': dt_bias,
        'gdn_norm_w': gain(ks[21], (DEPTH, GDN_DV)),
        'na_rpb': nrm(ks[22], (DEPTH, NA_HEADS, 2 * WIN_H - 1, 2 * WIN_W - 1), 0.1),
        'w_up': nrm(ks[23], (DEPTH, D_MODEL, 2 * D_FF), D_MODEL ** -0.5),
        'ffn_conv_w': nrm(ks[24], (DEPTH, FFN_K, 2 * D_FF), FFN_K ** -0.5),
        'ffn_conv_b': nrm(ks[25], (DEPTH, 2 * D_FF), 0.01),
        'w_down': nrm(ks[26], (DEPTH, D_FF, D_MODEL), D_FF ** -0.5),
    }


def reference(x_prompt, x_sample, cache_attn_kv, state_delta, c, c_ctx,
              w_ada, b_ada, g_pre_mix, g_post_mix, g_pre_ffn, g_post_ffn,
              w_in, w_out, conv_w, conv_b, conv_ln_g, conv_ln_b,
              gdn_conv_w, gdn_a_log, gdn_dt_bias, gdn_norm_w, na_rpb,
              w_up, ffn_conv_w, ffn_conv_b, w_down):
    cvec_ctx = c_ctx[None, None, :]
    cvec_lat = c[:, None, :]
    xp, xs = x_prompt, x_sample
    kv_new, st_new = [], []
    for l in range(DEPTH):
        lp = {
            'w_ada': w_ada[l], 'b_ada': b_ada[l],
            'g_pre_mix': g_pre_mix[l], 'g_post_mix': g_post_mix[l],
            'g_pre_ffn': g_pre_ffn[l], 'g_post_ffn': g_post_ffn[l],
            'w_in': w_in[l], 'w_out': w_out[l],
            'conv_w': conv_w[l], 'conv_b': conv_b[l], 'conv_ln_g': conv_ln_g[l], 'conv_ln_b': conv_ln_b[l],
            'gdn_conv_w': gdn_conv_w[l], 'gdn_a_log': gdn_a_log[l], 'gdn_dt_bias': gdn_dt_bias[l],
            'gdn_norm_w': gdn_norm_w[l], 'na_rpb': na_rpb[l],
            'w_up': w_up[l], 'ffn_conv_w': ffn_conv_w[l], 'ffn_conv_b': ffn_conv_b[l], 'w_down': w_down[l],
        }
        xp, (k_h, v_h, s_f, s_b) = trunk_layer(xp, cvec_ctx, lp, None)
        kv_new.append(jnp.stack([k_h, v_h], axis=1))
        st_new.append(jnp.stack([s_f, s_b], axis=1).astype(x_prompt.dtype))
        ctx = (cache_attn_kv[:, l, 0], cache_attn_kv[:, l, 1],
               state_delta[:, l, 0].astype(jnp.float32), state_delta[:, l, 1].astype(jnp.float32))
        xs, _ = trunk_layer(xs, cvec_lat, lp, ctx)
    new_cache_attn_kv = jnp.stack(kv_new, axis=1)
    new_state_delta = jnp.stack(st_new, axis=1)
    return (xp, xs, new_cache_attn_kv, new_state_delta)
```

```python
import functools
import math

import jax
import jax.numpy as jnp
from jax import lax
from jax.experimental import pallas as pl
from jax.experimental.pallas import tpu as pltpu

F32 = jnp.float32
BF16 = jnp.bfloat16
EPS = 1e-6
NEG = -1e30

D_MODEL = 1024
CONV_CH = 256
CONV_K = 31
GDN_HEADS = 4
GDN_D = 128
NA_HEADS = 4
NA_DIM = 64
GRID_W = 64
WIN_H = 8
WIN_W = 16
D_FF = 2816
CHUNK = 64
GDN_BLOCK = 256
NA_QROWS = 4
NA_WROWS = 12

COL_AV, COL_AG = 0, 256
COL_Q, COL_K, COL_V, COL_Z = 512, 1024, 1536, 2048
COL_QC, COL_KC, COL_VC = 2560, 2816, 3072
COL_BD = 3328
P_DIM = 3584

VMEM_LIMIT = 56 * 1024 * 1024


def _cparams(sem, vmem=None):
    return pltpu.CompilerParams(dimension_semantics=sem, vmem_limit_bytes=vmem)


def _const_spec(shape, index_map):
    return pl.BlockSpec(shape, index_map, pipeline_mode=pl.Buffered(1))


def _silu(x):
    return x * jax.nn.sigmoid(x)


def _rms(x, g):
    return x * lax.rsqrt(jnp.mean(x * x, axis=-1, keepdims=True) + EPS) * g


def _dot(a, b):
    return jnp.dot(a, b, preferred_element_type=F32)


def _dot_nt(a, b):
    return lax.dot_general(a, b, (((1,), (1,)), ((), ())), preferred_element_type=F32)


def _mm(a, b):
    return _dot(a.astype(BF16), b.astype(BF16))


def _dot_tn(a, b):
    return lax.dot_general(a, b, (((0,), (0,)), ((), ())), preferred_element_type=F32)


def _ada_kernel(c_ref, w_ref, b_ref, o_ref):
    s = _silu(c_ref[...]).astype(BF16)
    o_ref[0, 0] = _dot(s, w_ref[0].astype(BF16)) + b_ref[0, 0]


def ada_modulation(cvecs, w_ada, b_ada):
    depth = w_ada.shape[0]
    n = cvecs.shape[0]
    b4 = b_ada.reshape(depth, 6, 1, D_MODEL)
    return pl.pallas_call(
        _ada_kernel,
        out_shape=jax.ShapeDtypeStruct((depth, 6, n, D_MODEL), F32),
        grid=(depth, 6),
        in_specs=[pl.BlockSpec((n, D_MODEL), lambda l, j: (0, 0)),
                  pl.BlockSpec((1, D_MODEL, D_MODEL), lambda l, j: (l, 0, j)),
                  pl.BlockSpec((1, 1, 1, D_MODEL), lambda l, j: (l, j, 0, 0))],
        out_specs=pl.BlockSpec((1, 1, n, D_MODEL), lambda l, j: (l, j, 0, 0)),
        compiler_params=_cparams(("arbitrary", "arbitrary")),
        name="ada_modulation",
    )(cvecs, w_ada, b4)


def _in_kernel(x_ref, mod_ref, g_ref, w_ref, o_ref, *, row0, tiles_per_seq, n_chunk):
    row = row0 + pl.program_id(0) // tiles_per_seq
    sh = mod_ref[0, 0, pl.ds(row, 1), :]
    sc = mod_ref[0, 1, pl.ds(row, 1), :]
    h = _rms(x_ref[...], g_ref[...]) * (1.0 + sc) + sh
    hb = h.astype(BF16)
    for c in range(P_DIM // n_chunk):
        o_ref[:, c * n_chunk:(c + 1) * n_chunk] = _dot(hb, w_ref[:, c * n_chunk:(c + 1) * n_chunk])


def in_projection(x, mod, layer, g_pre, w_in, *, row0, per_seq, tm):
    t = x.shape[0]
    kern = functools.partial(_in_kernel, row0=row0, tiles_per_seq=per_seq // tm, n_chunk=512)
    return pl.pallas_call(
        kern,
        out_shape=jax.ShapeDtypeStruct((t, P_DIM), F32),
        grid=(t // tm,),
        in_specs=[pl.BlockSpec((tm, D_MODEL), lambda i: (i, 0)),
                  _const_spec((1, 6, mod.shape[2], D_MODEL), lambda i: (layer, 0, 0, 0)),
                  _const_spec((1, D_MODEL), lambda i: (0, 0)),
                  _const_spec((D_MODEL, P_DIM), lambda i: (0, 0))],
        out_specs=pl.BlockSpec((tm, P_DIM), lambda i: (i, 0)),
        compiler_params=_cparams(("parallel",), VMEM_LIMIT),
        name="in_projection",
    )(x, mod, g_pre, w_in)


CONF_HALO = 16
CONF_ROWS = 64


def _conf_kernel(av_ref, ag_ref, avp_ref, agp_ref, avn_ref, agn_ref, w_ref, b_ref, lg_ref, lb_ref,
                 o_ref, u_scr, *, tl, nt):
    t = pl.program_id(1)

    def glu(a_ref, g_ref):
        return a_ref[0] * jax.nn.sigmoid(g_ref[0])

    u_scr[0:CONF_HALO, :] = jnp.where(t > 0, glu(avp_ref, agp_ref), 0.0)
    u_scr[CONF_HALO:CONF_HALO + tl, :] = glu(av_ref, ag_ref)
    u_scr[CONF_HALO + tl:2 * CONF_HALO + tl, :] = jnp.where(t < nt - 1, glu(avn_ref, agn_ref), 0.0)
    half = CONV_K // 2
    for r0 in range(0, tl, CONF_ROWS):
        acc = jnp.zeros((CONF_ROWS, CONV_CH), F32) + b_ref[...]
        for k in range(CONV_K):
            acc = acc + w_ref[pl.ds(k, 1), :] * u_scr[pl.ds(r0 + k + CONF_HALO - half, CONF_ROWS), :]
        mu = jnp.mean(acc, axis=-1, keepdims=True)
        cen = acc - mu
        var = jnp.mean(cen * cen, axis=-1, keepdims=True)
        y = cen * lax.rsqrt(var + EPS) * lg_ref[...] + lb_ref[...]
        o_ref[0, r0:r0 + CONF_ROWS, :] = _silu(y).astype(BF16)


def conformer_conv(p3, conv_w, conv_b, ln_g, ln_b, *, tl):
    b, l, _ = p3.shape
    nt = l // tl
    hb = tl // CONF_HALO
    last = l // CONF_HALO - 1
    kern = functools.partial(_conf_kernel, tl=tl, nt=nt)

    def main(col):
        return pl.BlockSpec((1, tl, CONV_CH), lambda i, t: (i, t, col))

    def prev(col):
        return pl.BlockSpec((1, CONF_HALO, CONV_CH), lambda i, t: (i, jnp.maximum(t * hb - 1, 0), col))

    def nxt(col):
        return pl.BlockSpec((1, CONF_HALO, CONV_CH), lambda i, t: (i, jnp.minimum((t + 1) * hb, last), col))

    cav, cag = COL_AV // CONV_CH, COL_AG // CONV_CH
    vec = lambda n: _const_spec((n, CONV_CH), lambda i, t: (0, 0))
    return pl.pallas_call(
        kern,
        out_shape=jax.ShapeDtypeStruct((b, l, CONV_CH), BF16),
        grid=(b, nt),
        in_specs=[main(cav), main(cag), prev(cav), prev(cag), nxt(cav), nxt(cag),
                  vec(CONV_K), vec(1), vec(1), vec(1)],
        out_specs=pl.BlockSpec((1, tl, CONV_CH), lambda i, t: (i, t, 0)),
        scratch_shapes=[pltpu.VMEM((tl + 2 * CONF_HALO, CONV_CH), F32)],
        compiler_params=_cparams(("parallel", "parallel")),
        name="conformer_conv",
    )(p3, p3, p3, p3, p3, p3, conv_w, conv_b, ln_g, ln_b)


GP_HALO = 8
BG_ROWS = 24


def _gprep_kernel(q_ref, k_ref, v_ref, qp_ref, kp_ref, vp_ref, qn_ref, kn_ref, vn_ref, bd_ref,
                  cw_ref, alog_ref, dtb_ref, qo_ref, ko_ref, vo_ref, bg_ref, bgt_ref, *, tl, nt):
    t = pl.program_id(1)
    gw = GDN_HEADS * GDN_D
    rows = lax.broadcasted_iota(jnp.int32, (tl, gw), 0)

    def conv_silu(x_ref, xp_ref, xn_ref, j):
        x = x_ref[0]
        prev_row = jnp.where(t > 0, xp_ref[0, GP_HALO - 1:GP_HALO, :], 0.0)
        next_row = jnp.where(t < nt - 1, xn_ref[0, 0:1, :], 0.0)
        up = jnp.where(rows == 0, prev_row, pltpu.roll(x, 1, 0))
        dn = jnp.where(rows == tl - 1, next_row, pltpu.roll(x, tl - 1, 0))
        w = cw_ref[:, j * gw:(j + 1) * gw]
        return _silu(up * w[0:1, :] + x * w[1:2, :] + dn * w[2:3, :])

    def l2n(seg, scale):
        return seg * (lax.rsqrt(jnp.sum(seg * seg, axis=-1, keepdims=True) + EPS) * scale)

    q = conv_silu(q_ref, qp_ref, qn_ref, 0)
    k = conv_silu(k_ref, kp_ref, kn_ref, 1)
    vo_ref[0] = conv_silu(v_ref, vp_ref, vn_ref, 2)
    for h in range(GDN_HEADS):
        sl = slice(h * GDN_D, (h + 1) * GDN_D)
        qo_ref[0, :, sl] = l2n(q[:, sl], GDN_D ** -0.5)
        ko_ref[0, :, sl] = l2n(k[:, sl], 1.0)

    bd = bd_ref[0]
    lane = lax.broadcasted_iota(jnp.int32, (tl, 128), 1)
    rc = lax.broadcasted_iota(jnp.int32, (tl, 128), 0) & (CHUNK - 1)
    is_g = (lane >= 8) & (lane < 16)
    beta = jax.nn.sigmoid(bd)
    xg = bd + dtb_ref[...]
    softplus = jnp.maximum(xg, 0.0) + jnp.log1p(jnp.exp(-jnp.abs(xg)))
    g = jnp.where(is_g, -jnp.exp(alog_ref[...]) * softplus, 0.0)
    pre = g
    suf = g
    s = 1
    while s < CHUNK:
        pre = pre + jnp.where(rc >= s, pltpu.roll(pre, s, 0), 0.0)
        suf = suf + jnp.where(rc < CHUNK - s, pltpu.roll(suf, tl - s, 0), 0.0)
        s *= 2
    tot = pre + suf - g
    bg = jnp.where(lane < 8, beta,
                   jnp.where(lane < 12, pre,
                             jnp.where(lane < 16, suf, pltpu.roll(tot, 8, 1))))
    bg = jnp.where(lane < 24, bg, 0.0)
    bg_ref[0] = bg
    bgt_ref[0] = jnp.transpose(bg)[0:BG_ROWS, :]


def gdn_prepare(p3, conv_w, alog_row, dtb_row, *, tl):
    b, l, _ = p3.shape
    nt = l // tl
    gw = GDN_HEADS * GDN_D
    hb = tl // GP_HALO
    last = l // GP_HALO - 1
    kern = functools.partial(_gprep_kernel, tl=tl, nt=nt)

    def main(col):
        return pl.BlockSpec((1, tl, gw), lambda i, t: (i, t, col))

    def prev(col):
        return pl.BlockSpec((1, GP_HALO, gw), lambda i, t: (i, jnp.maximum(t * hb - 1, 0), col))

    def nxt(col):
        return pl.BlockSpec((1, GP_HALO, gw), lambda i, t: (i, jnp.minimum((t + 1) * hb, last), col))

    cq, ck, cv = COL_Q // gw, COL_K // gw, COL_V // gw
    seq = jax.ShapeDtypeStruct((b, l, gw), F32)
    return pl.pallas_call(
        kern,
        out_shape=(seq, seq, seq, jax.ShapeDtypeStruct((b, l, 128), F32),
                   jax.ShapeDtypeStruct((b, BG_ROWS, l), F32)),
        grid=(b, nt),
        in_specs=[main(cq), main(ck), main(cv), prev(cq), prev(ck), prev(cv), nxt(cq), nxt(ck), nxt(cv),
                  pl.BlockSpec((1, tl, 128), lambda i, t: (i, t, COL_BD // 128)),
                  _const_spec((3, 3 * gw), lambda i, t: (0, 0)),
                  _const_spec((1, 128), lambda i, t: (0, 0)),
                  _const_spec((1, 128), lambda i, t: (0, 0))],
        out_specs=(main(0), main(0), main(0),
                   pl.BlockSpec((1, tl, 128), lambda i, t: (i, t, 0)),
                   pl.BlockSpec((1, BG_ROWS, tl), lambda i, t: (i, 0, t))),
        compiler_params=_cparams(("parallel", "parallel")),
        name="gdn_prepare",
    )(p3, p3, p3, p3, p3, p3, p3, p3, p3, p3, conv_w, alog_row, dtb_row)


def _gscan_kernel(qf_ref, kf_ref, vf_ref, bgf_ref, bgtf_ref, qb_ref, kb_ref, vb_ref, bgb_ref, bgtb_ref,
                  s0_ref, of_ref, ob_ref, sfin_ref, s_scr, vn_scr, *, r):
    nb = pl.program_id(1)
    nchunk = r // CHUNK
    shift = int(math.log2(CHUNK))

    @pl.when(nb == 0)
    def _():
        s_scr[...] = s0_ref[0, 0]

    vn_scr[...] = jnp.zeros((r, GDN_D), F32)
    ii = lax.broadcasted_iota(jnp.int32, (r, r), 0)
    jj = lax.broadcasted_iota(jnp.int32, (r, r), 1)
    blk_same = {s: (ii >> s) == (jj >> s) for s in range(3, shift + 1)}
    same = blk_same[shift]
    dirs = ((qf_ref, kf_ref, vf_ref, bgf_ref, bgtf_ref, of_ref), (qb_ref, kb_ref, vb_ref, bgb_ref, bgtb_ref, ob_ref))
    for d, (q_ref, k_ref, v_ref, bg_ref, bgt_ref, o_ref) in enumerate(dirs):
        if d == 0:
            m_c, m_s = same & (ii >= jj), same & (ii > jj)
            order = list(range(nchunk))
        else:
            m_c, m_s = same & (ii <= jj), same & (ii < jj)
            order = list(range(nchunk - 1, -1, -1))
        for h in range(GDN_HEADS):
            j = d * GDN_HEADS + h
            sl_h = slice(h * GDN_D, (h + 1) * GDN_D)
            beta = bg_ref[0, :, j:j + 1]
            gcc = bg_ref[0, :, 8 + j:9 + j]
            gl = bg_ref[0, :, 16 + j:17 + j]
            gcr = bgt_ref[0, 8 + j:9 + j, :]
            q = q_ref[0, :, sl_h]
            k = k_ref[0, :, sl_h]
            v = v_ref[0, :, sl_h]
            decay = jnp.exp(jnp.where(m_c, gcc - gcr, NEG))
            kbeta = k * beta
            kb16 = k.astype(BF16)
            lm = jnp.where(m_s, _dot_nt(kbeta.astype(BF16), kb16) * decay, 0.0)
            egc = jnp.exp(gcc)
            x = jnp.concatenate([v * beta, kbeta * egc], axis=1)
            l8 = jnp.where(blk_same[3], lm, 0.0)
            m2 = _mm(l8, l8)
            m4 = _mm(m2, m2)
            t = (ii == jj).astype(F32) - l8
            t = t + _mm(t, m2)
            t = t + _mm(t, m4)
            for lvl in range(3, shift):
                c_s = jnp.where(blk_same[lvl + 1] & jnp.logical_not(blk_same[lvl]), lm, 0.0)
                t = t - _mm(t, _mm(c_s, t))
            x = _mm(t, x)
            u = x[:, :GDN_D]
            w = x[:, GDN_D:]
            intra = (_dot_nt(q.astype(BF16), kb16) * decay).astype(BF16)
            qd = q * egc
            kd = (k * jnp.exp(gl - gcc)).astype(BF16)
            eg_last = jnp.exp(jnp.broadcast_to(gl, (r, GDN_D)))
            state = s_scr[d, h]
            outs = [None] * nchunk
            for c in order:
                sl = slice(c * CHUNK, (c + 1) * CHUNK)
                wq = jnp.concatenate([w[sl], qd[sl]], axis=0).astype(BF16)
                a = _dot(wq, state.astype(BF16))
                v_new = u[sl] - a[:CHUNK]
                vn_scr[sl, :] = v_new
                outs[c] = a[CHUNK:] + _dot(intra[sl, :], vn_scr[...].astype(BF16))
                state = state * eg_last[c * CHUNK:c * CHUNK + 1, :] + _dot_tn(kd[sl], v_new.astype(BF16))
            s_scr[d, h] = state
            o_ref[0, :, sl_h] = jnp.concatenate(outs, axis=0)
    sfin_ref[0] = s_scr[...]


def gdn_scan(q, k, v, bg, bgt, state0, layer):
    b, l, gw = q.shape
    r = min(GDN_BLOCK, l)
    nbk = l // r
    kern = functools.partial(_gscan_kernel, r=r)
    fwd = lambda i, n: (i, n, 0)
    bwd = lambda i, n: (i, nbk - 1 - n, 0)
    fwd_t = lambda i, n: (i, 0, n)
    bwd_t = lambda i, n: (i, 0, nbk - 1 - n)
    seq = lambda im: pl.BlockSpec((1, r, gw), im)
    st_shape = (2, GDN_HEADS, GDN_D, GDN_D)
    return pl.pallas_call(
        kern,
        out_shape=(jax.ShapeDtypeStruct((b, l, gw), F32), jax.ShapeDtypeStruct((b, l, gw), F32),
                   jax.ShapeDtypeStruct((b,) + st_shape, F32)),
        grid=(b, nbk),
        in_specs=[seq(fwd), seq(fwd), seq(fwd), pl.BlockSpec((1, r, 128), fwd), pl.BlockSpec((1, BG_ROWS, r), fwd_t),
                  seq(bwd), seq(bwd), seq(bwd), pl.BlockSpec((1, r, 128), bwd), pl.BlockSpec((1, BG_ROWS, r), bwd_t),
                  pl.BlockSpec((1, 1) + st_shape, lambda i, n: (i, layer, 0, 0, 0, 0))],
        out_specs=(seq(fwd), seq(bwd), pl.BlockSpec((1,) + st_shape, lambda i, n: (i, 0, 0, 0, 0))),
        scratch_shapes=[pltpu.VMEM(st_shape, F32), pltpu.VMEM((r, GDN_D), F32)],
        compiler_params=_cparams(("parallel", "arbitrary"), VMEM_LIMIT),
        name="gdn_scan",
    )(q, k, v, bg, bgt, q, k, v, bg, bgt, state0)


def _softmax_pv(q, keys, vals, biases, acc_shape):
    lane = lax.broadcasted_iota(jnp.int32, (1, NA_HEADS * NA_DIM), 1)
    acc = jnp.zeros(acc_shape, F32)
    for h in range(NA_HEADS):
        hm = (lane >= h * NA_DIM) & (lane < (h + 1) * NA_DIM)
        s = _dot_nt(jnp.where(hm, q, 0.0).astype(BF16), keys)
        if biases is not None:
            s = s + biases(h)
        mx = jnp.max(s, axis=-1, keepdims=True)
        e = jnp.exp(s - mx)
        p = e * (1.0 / jnp.sum(e, axis=-1, keepdims=True))
        acc = acc + _dot(p.astype(BF16), jnp.where(hm, vals, 0.0).astype(BF16))
    return acc


def _attn_ctx_kernel(q_ref, k_ref, v_ref, o_ref):
    q = q_ref[0] * (NA_DIM ** -0.5)
    acc = _softmax_pv(q, k_ref[0].astype(BF16), v_ref[0], None, q.shape)
    o_ref[0] = acc.astype(BF16)


def context_attention(p3):
    b, l, _ = p3.shape
    w = NA_HEADS * NA_DIM
    col = lambda c: pl.BlockSpec((1, l, w), lambda i: (i, 0, c // w))
    return pl.pallas_call(
        _attn_ctx_kernel,
        out_shape=jax.ShapeDtypeStruct((b, l, w), BF16),
        grid=(b,),
        in_specs=[col(COL_QC), col(COL_KC), col(COL_VC)],
        out_specs=pl.BlockSpec((1, l, w), lambda i: (i, 0, 0)),
        compiler_params=_cparams(("parallel",)),
        name="context_attention",
    )(p3, p3, p3)


def _attn_lat_kernel(q_ref, k_ref, v_ref, kc_ref, vc_ref, bias_ref, o_ref, *, rows):
    i = pl.program_id(1)
    nwin = NA_WROWS * GRID_W
    ws = jnp.clip(i * NA_QROWS - WIN_H // 2, 0, rows - NA_WROWS)
    start = pl.multiple_of(ws * GRID_W, GRID_W)
    keys = jnp.concatenate([k_ref[0, pl.ds(start, nwin), :], kc_ref[0]], axis=0).astype(BF16)
    vals = jnp.concatenate([v_ref[0, pl.ds(start, nwin), :], vc_ref[0]], axis=0)
    nctx = kc_ref.shape[1]
    q = q_ref[0] * (NA_DIM ** -0.5)

    def bias(h):
        return jnp.concatenate([bias_ref[0, h], jnp.zeros((q.shape[0], nctx), F32)], axis=1)

    o_ref[0] = _softmax_pv(q, keys, vals, bias, q.shape).astype(BF16)


def neighbourhood_attention(p3, k_ctx, v_ctx, bias_tab):
    b, n, _ = p3.shape
    rows = n // GRID_W
    w = NA_HEADS * NA_DIM
    tq = NA_QROWS * GRID_W
    ni = rows // NA_QROWS
    nctx = k_ctx.shape[1]
    kern = functools.partial(_attn_lat_kernel, rows=rows)
    case = lambda i, t: (jnp.where(t == 0, 0, jnp.where(t == ni - 1, 2, 1)), 0, 0, 0)
    return pl.pallas_call(
        kern,
        out_shape=jax.ShapeDtypeStruct((b, n, w), BF16),
        grid=(b, ni),
        in_specs=[pl.BlockSpec((1, tq, w), lambda i, t: (i, t, COL_QC // w)),
                  pl.BlockSpec((1, n, w), lambda i, t: (i, 0, COL_KC // w)),
                  pl.BlockSpec((1, n, w), lambda i, t: (i, 0, COL_VC // w)),
                  pl.BlockSpec((1, nctx, w), lambda i, t: (i, 0, 0)),
                  pl.BlockSpec((1, nctx, w), lambda i, t: (i, 0, 0)),
                  pl.BlockSpec((1, NA_HEADS, tq, NA_WROWS * GRID_W), case)],
        out_specs=pl.BlockSpec((1, tq, w), lambda i, t: (i, t, 0)),
        compiler_params=_cparams(("parallel", "arbitrary"), VMEM_LIMIT),
        name="neighbourhood_attention",
    )(p3, p3, p3, k_ctx, v_ctx, bias_tab)


def _bias_kernel(rpb_ref, o_ref, *, rows):
    l = pl.program_id(0)
    h = pl.program_id(1)
    n_dr, n_dc = 2 * WIN_H - 1, 2 * WIN_W - 1
    base = (l * NA_HEADS + h) * (n_dr * n_dc)
    c = lax.broadcasted_iota(jnp.int32, (GRID_W, 2 * GRID_W), 0)
    lane = lax.broadcasted_iota(jnp.int32, (GRID_W, 2 * GRID_W), 1)
    kc = lane & (GRID_W - 1)
    dc = jnp.clip(kc - c + (WIN_W - 1), 0, n_dc - 1)
    cs = jnp.clip(c - WIN_W // 2, 0, GRID_W - WIN_W)
    col_ok = (kc >= cs) & (kc < cs + WIN_W)
    neg = jnp.full((GRID_W, 2 * GRID_W), NEG, F32)
    tabs = []
    for a in range(n_dr):
        t = jnp.zeros((GRID_W, 2 * GRID_W), F32)
        for bb in range(n_dc):
            t = jnp.where(dc == bb, rpb_ref[base + a * n_dc + bb], t)
        tabs.append(jnp.where(col_ok, t, NEG))
    interior = 2 * NA_QROWS
    cases = ((0, 0), (interior, interior - WIN_H // 2), (rows - NA_QROWS, rows - NA_WROWS))
    for case, (r0, ws) in enumerate(cases):
        for iq in range(NA_QROWS):
            r = r0 + iq
            rs = min(max(r - WIN_H // 2, 0), rows - WIN_H)
            for jp in range(NA_WROWS // 2):
                halves = []
                for half in range(2):
                    kr = ws + 2 * jp + half
                    halves.append(tabs[kr - r + WIN_H - 1] if rs <= kr < rs + WIN_H else neg)
                blk = jnp.where(lane < GRID_W, halves[0], halves[1])
                o_ref[0, case, 0, iq * GRID_W:(iq + 1) * GRID_W, jp * 2 * GRID_W:(jp + 1) * 2 * GRID_W] = blk


def attention_bias_tables(na_rpb, rows):
    depth = na_rpb.shape[0]
    kern = functools.partial(_bias_kernel, rows=rows)
    tq, nw = NA_QROWS * GRID_W, NA_WROWS * GRID_W
    return pl.pallas_call(
        kern,
        out_shape=jax.ShapeDtypeStruct((depth, 3, NA_HEADS, tq, nw), F32),
        grid=(depth, NA_HEADS),
        in_specs=[pl.BlockSpec(memory_space=pltpu.SMEM)],
        out_specs=pl.BlockSpec((1, 3, 1, tq, nw), lambda l, h: (l, 0, h, 0, 0)),
        compiler_params=_cparams(("arbitrary", "arbitrary")),
        name="attention_bias_tables",
    )(na_rpb.reshape(-1))


def _out_kernel(x_ref, oa_ref, of_ref, ob_ref, z_ref, oc_ref, w_ref, mod_ref, gpost_ref, gpre_ref, gnw_ref,
                x1_ref, h2_ref, *, row0, tiles_per_seq):
    row = row0 + pl.program_id(0) // tiles_per_seq
    g1 = mod_ref[0, 2, pl.ds(row, 1), :]
    sh2 = mod_ref[0, 3, pl.ds(row, 1), :]
    sc2 = mod_ref[0, 4, pl.ds(row, 1), :]
    m = _dot(oa_ref[...], w_ref[0:CONV_CH, :])
    og = of_ref[...] + ob_ref[...]
    z = z_ref[...]
    for h in range(GDN_HEADS):
        sl = slice(h * GDN_D, (h + 1) * GDN_D)
        gated = _rms(og[:, sl], gnw_ref[...]) * _silu(z[:, sl])
        m = m + _dot(gated.astype(BF16), w_ref[CONV_CH + h * GDN_D:CONV_CH + (h + 1) * GDN_D, :])
    m = m + _dot(oc_ref[...], w_ref[CONV_CH + GDN_HEADS * GDN_D:, :])
    x1 = x_ref[...] + g1 * _rms(m, gpost_ref[...])
    x1_ref[...] = x1
    h2_ref[...] = (_rms(x1, gpre_ref[...]) * (1.0 + sc2) + sh2).astype(BF16)


def out_projection(x, oa, of, ob, p, oc, w_out, mod, layer, g_post, g_pre_ffn, gnw, *, row0, per_seq, tm):
    t = x.shape[0]
    gw = GDN_HEADS * GDN_D
    kern = functools.partial(_out_kernel, row0=row0, tiles_per_seq=per_seq // tm)
    rowblk = lambda n: pl.BlockSpec((tm, n), lambda i: (i, 0))
    vec = lambda n: _const_spec((1, n), lambda i: (0, 0))
    return pl.pallas_call(
        kern,
        out_shape=(jax.ShapeDtypeStruct((t, D_MODEL), F32), jax.ShapeDtypeStruct((t, D_MODEL), BF16)),
        grid=(t // tm,),
        in_specs=[rowblk(D_MODEL), rowblk(CONV_CH), rowblk(gw), rowblk(gw),
                  pl.BlockSpec((tm, gw), lambda i: (i, COL_Z // gw)), rowblk(NA_HEADS * NA_DIM),
                  _const_spec((D_MODEL, D_MODEL), lambda i: (0, 0)),
                  _const_spec((1, 6, mod.shape[2], D_MODEL), lambda i: (layer, 0, 0, 0)),
                  vec(D_MODEL), vec(D_MODEL), vec(GDN_D)],
        out_specs=(rowblk(D_MODEL), rowblk(D_MODEL)),
        compiler_params=_cparams(("parallel",), VMEM_LIMIT),
        name="out_projection",
    )(x, oa, of, ob, p, oc, w_out, mod, g_post, g_pre_ffn, gnw)


FFN_HALO = 16
FFN_CHUNK = 256


def _ffn_kernel(h_ref, hp_ref, hn_ref, x_ref, wg_ref, wv_ref, wd_ref, cwg_ref, cwv_ref, cbg_ref, cbv_ref,
                mod_ref, gpost_ref, o_ref, hext_scr, acc_scr, *, row0, tiles_per_seq, tm, seq_len):
    i = pl.program_id(0)
    row = row0 + i // tiles_per_seq
    g2 = mod_ref[0, 5, pl.ds(row, 1), :]
    hext_scr[0:tm, :] = h_ref[...]
    hext_scr[tm:tm + FFN_HALO, :] = hp_ref[...]
    hext_scr[tm + FFN_HALO:tm + 2 * FFN_HALO, :] = hn_ref[...]
    hext = hext_scr[...]
    rows = lax.broadcasted_iota(jnp.int32, (tm, FFN_CHUNK), 0)
    pos = (i * tm + rows) & (seq_len - 1)
    first = pos == 0
    last = pos == seq_len - 1

    def conv(u_ext, cw_ref, cb_ref, sl):
        u = u_ext[0:tm]
        prev_row = u_ext[tm + FFN_HALO - 1:tm + FFN_HALO]
        next_row = u_ext[tm + FFN_HALO:tm + FFN_HALO + 1]
        up = jnp.where(rows == 0, prev_row, pltpu.roll(u, 1, 0))
        dn = jnp.where(rows == tm - 1, next_row, pltpu.roll(u, tm - 1, 0))
        up = jnp.where(first, 0.0, up)
        dn = jnp.where(last, 0.0, dn)
        return up * cw_ref[0:1, sl] + u * cw_ref[1:2, sl] + dn * cw_ref[2:3, sl] + cb_ref[:, sl]

    for c in range(D_FF // FFN_CHUNK):
        sl = slice(c * FFN_CHUNK, (c + 1) * FFN_CHUNK)
        gate = conv(_dot(hext, wg_ref[:, sl]), cwg_ref, cbg_ref, sl)
        val = conv(_dot(hext, wv_ref[:, sl]), cwv_ref, cbv_ref, sl)
        act = (_silu(gate) * val).astype(BF16)
        contrib = _dot(act, wd_ref[sl, :])
        if c == 0:
            acc_scr[...] = contrib
        else:
            acc_scr[...] += contrib
    o_ref[...] = x_ref[...] + g2 * _rms(acc_scr[...], gpost_ref[...])


def conv_ffn(h2, x1, wg, wv, wd, cwg, cwv, cbg, cbv, mod, layer, g_post, *, row0, per_seq, tm, seq_len):
    t = x1.shape[0]
    hb = tm // FFN_HALO
    last = t // FFN_HALO - 1
    kern = functools.partial(_ffn_kernel, row0=row0, tiles_per_seq=per_seq // tm, tm=tm, seq_len=seq_len)
    rowblk = lambda n: pl.BlockSpec((tm, n), lambda i: (i, 0))
    return pl.pallas_call(
        kern,
        out_shape=jax.ShapeDtypeStruct((t, D_MODEL), F32),
        grid=(t // tm,),
        in_specs=[rowblk(D_MODEL),
                  pl.BlockSpec((FFN_HALO, D_MODEL), lambda i: (jnp.maximum(i * hb - 1, 0), 0)),
                  pl.BlockSpec((FFN_HALO, D_MODEL), lambda i: (jnp.minimum((i + 1) * hb, last), 0)),
                  rowblk(D_MODEL),
                  _const_spec((D_MODEL, D_FF), lambda i: (0, 0)),
                  _const_spec((D_MODEL, D_FF), lambda i: (0, 0)),
                  _const_spec((D_FF, D_MODEL), lambda i: (0, 0)),
                  _const_spec((3, D_FF), lambda i: (0, 0)),
                  _const_spec((3, D_FF), lambda i: (0, 0)),
                  _const_spec((1, D_FF), lambda i: (0, 0)),
                  _const_spec((1, D_FF), lambda i: (0, 0)),
                  _const_spec((1, 6, mod.shape[2], D_MODEL), lambda i: (layer, 0, 0, 0)),
                  _const_spec((1, D_MODEL), lambda i: (0, 0))],
        out_specs=rowblk(D_MODEL),
        scratch_shapes=[pltpu.VMEM((tm + 2 * FFN_HALO, D_MODEL), BF16), pltpu.VMEM((tm, D_MODEL), F32)],
        compiler_params=_cparams(("parallel",), VMEM_LIMIT),
        name="conv_ffn",
    )(h2, h2, h2, x1, wg, wv, wd, cwg, cwv, cbg, cbv, mod, g_post)


def _trunk_layer(x, lw, layer, mod, *, batch, seq, row0, per_seq, ctx):
    tm = min(512, seq)
    p = in_projection(x, mod, layer, lw["g_pre_mix"], lw["w_in"], row0=row0, per_seq=per_seq, tm=min(256, seq))
    p3 = p.reshape(batch, seq, P_DIM)
    oa = conformer_conv(p3, lw["conv_w"], lw["conv_b"], lw["conv_ln_g"], lw["conv_ln_b"], tl=min(512, seq))
    qn, kn, vn, bg, bgt = gdn_prepare(p3, lw["gdn_conv_w"], lw["alog_row"], lw["dtb_row"], tl=min(256, seq))
    if ctx is None:
        state0 = jnp.zeros((batch, 1, 2, GDN_HEADS, GDN_D, GDN_D), F32)
        of, ob, s_fin = gdn_scan(qn, kn, vn, bg, bgt, state0, 0)
        oc = context_attention(p3)
    else:
        k_ctx, v_ctx, bias_tab, state_delta = ctx
        of, ob, s_fin = gdn_scan(qn, kn, vn, bg, bgt, state_delta, layer)
        oc = neighbourhood_attention(p3, k_ctx, v_ctx, bias_tab)
    t = batch * seq
    gw = GDN_HEADS * GDN_D
    x1, h2 = out_projection(x, oa.reshape(t, CONV_CH), of.reshape(t, gw), ob.reshape(t, gw), p,
                            oc.reshape(t, NA_HEADS * NA_DIM), lw["w_out"], mod, layer, lw["g_post_mix"],
                            lw["g_pre_ffn"], lw["gdn_norm_w"], row0=row0, per_seq=per_seq, tm=tm)
    x2 = conv_ffn(h2, x1, lw["wg"], lw["wv"], lw["wd"], lw["cwg"], lw["cwv"], lw["cbg"], lw["cbv"], mod, layer,
                  lw["g_post_ffn"], row0=row0, per_seq=per_seq, tm=tm, seq_len=seq)
    return x2, p3, s_fin


def kernel(x_prompt, x_sample, cache_attn_kv, state_delta, c, c_ctx, w_ada, b_ada, g_pre_mix, g_post_mix, g_pre_ffn, g_post_ffn, w_in, w_out, conv_w, conv_b, conv_ln_g, conv_ln_b, gdn_conv_w, gdn_a_log, gdn_dt_bias, gdn_norm_w, na_rpb, w_up, ffn_conv_w, ffn_conv_b, w_down):
    depth = w_in.shape[0]
    bp, sp, d = x_prompt.shape
    bs, ss, _ = x_sample.shape
    rows = ss // GRID_W
    nmod = 16
    assert 1 + bs <= nmod

    n_gdn = 2 * GDN_HEADS
    c0 = 2 * CONV_CH + 4 * GDN_HEADS * GDN_D
    w_in_r = jnp.concatenate(
        [w_in[:, :, :c0], w_in[:, :, c0 + 2 * n_gdn:], w_in[:, :, c0:c0 + 2 * n_gdn],
         jnp.zeros((depth, d, P_DIM - w_in.shape[2]), w_in.dtype)], axis=-1).astype(BF16)
    w_out_b = w_out.astype(BF16)
    wg_b = w_up[:, :, :D_FF].astype(BF16)
    wv_b = w_up[:, :, D_FF:].astype(BF16)
    wd_b = w_down.astype(BF16)

    def lane_row(vals):
        return jnp.pad(vals.reshape(depth, 1, n_gdn), ((0, 0), (0, 0), (8, 128 - 8 - n_gdn)))

    alog_rows = lane_row(gdn_a_log)
    dtb_rows = lane_row(gdn_dt_bias)

    cvecs = jnp.concatenate([c_ctx[None, :], c, jnp.zeros((nmod - 1 - bs, d), F32)], axis=0)
    mod = ada_modulation(cvecs, w_ada, b_ada)
    bias_tabs = attention_bias_tables(na_rpb, rows)

    xp = x_prompt.reshape(bp * sp, d)
    xs = x_sample.reshape(bs * ss, d)
    kv_new, st_new = [], []
    hw = NA_HEADS * NA_DIM
    for l in range(depth):
        lw = {
            "g_pre_mix": g_pre_mix[l][None], "g_post_mix": g_post_mix[l][None],
            "g_pre_ffn": g_pre_ffn[l][None], "g_post_ffn": g_post_ffn[l][None],
            "w_in": w_in_r[l], "w_out": w_out_b[l],
            "conv_w": conv_w[l], "conv_b": conv_b[l][None], "conv_ln_g": conv_ln_g[l][None],
            "conv_ln_b": conv_ln_b[l][None],
            "gdn_conv_w": gdn_conv_w[l], "alog_row": alog_rows[l], "dtb_row": dtb_rows[l],
            "gdn_norm_w": gdn_norm_w[l][None],
            "wg": wg_b[l], "wv": wv_b[l], "wd": wd_b[l],
            "cwg": ffn_conv_w[l][:, :D_FF], "cwv": ffn_conv_w[l][:, D_FF:],
            "cbg": ffn_conv_b[l][None, :D_FF], "cbv": ffn_conv_b[l][None, D_FF:],
        }
        xp, p3, s_fin = _trunk_layer(xp, lw, l, mod, batch=bp, seq=sp, row0=0, per_seq=bp * sp, ctx=None)
        heads = lambda col: p3[:, :, col:col + hw].reshape(bp, sp, NA_HEADS, NA_DIM).transpose(0, 2, 1, 3)
        kv_new.append(jnp.stack([heads(COL_KC), heads(COL_VC)], axis=1))
        st_new.append(s_fin)
        lanes = lambda t: t.transpose(0, 2, 1, 3).reshape(bs, t.shape[2], hw)
        ctx = (lanes(cache_attn_kv[:, l, 0]), lanes(cache_attn_kv[:, l, 1]), bias_tabs[l], state_delta)
        xs, _, _ = _trunk_layer(xs, lw, l, mod, batch=bs, seq=ss, row0=1, per_seq=ss, ctx=ctx)
    return (xp.reshape(bp, sp, d), xs.reshape(bs, ss, d), jnp.stack(kv_new, axis=1), jnp.stack(st_new, axis=1))
```

```python
import functools
import math

import jax
import jax.numpy as jnp
from jax import lax
from jax.experimental import pallas as pl
from jax.experimental.pallas import tpu as pltpu

F32 = jnp.float32
BF16 = jnp.bfloat16
EPS = 1e-6
NEG = -1e30

D_MODEL = 1024
CONV_CH = 256
CONV_K = 31
GDN_HEADS = 4
GDN_D = 128
NA_HEADS = 4
NA_DIM = 64
GRID_W = 64
WIN_H = 8
WIN_W = 16
D_FF = 2816
CHUNK = 64
GDN_BLOCK = 256
NA_QROWS = 4
NA_WROWS = 12

COL_AV, COL_AG = 0, 256
COL_Q, COL_K, COL_V, COL_Z = 512, 1024, 1536, 2048
COL_QC, COL_KC, COL_VC = 2560, 2816, 3072
COL_BD = 3328
P_DIM = 3584

VMEM_LIMIT = 56 * 1024 * 1024


def _cparams(sem, vmem=None):
    return pltpu.CompilerParams(dimension_semantics=sem, vmem_limit_bytes=vmem)


def _const_spec(shape, index_map):
    return pl.BlockSpec(shape, index_map, pipeline_mode=pl.Buffered(1))


def _silu(x):
    return x * jax.nn.sigmoid(x)


def _rms(x, g):
    return x * lax.rsqrt(jnp.mean(x * x, axis=-1, keepdims=True) + EPS) * g


def _dot(a, b):
    return jnp.dot(a, b, preferred_element_type=F32)


def _dot_nt(a, b):
    return lax.dot_general(a, b, (((1,), (1,)), ((), ())), preferred_element_type=F32)


def _mm(a, b):
    return _dot(a.astype(BF16), b.astype(BF16))


def _half_rows(x, s, half):
    n, c = x.shape[0] // (2 * s), x.shape[1]
    return x.reshape(n, 2, s, c)[:, half].reshape(n * s, c)


def _expand_rows(y, s, half):
    n, c = y.shape[0] // s, y.shape[1]
    y4 = y.reshape(n, 1, s, c)
    z = jnp.zeros_like(y4)
    return jnp.concatenate([z, y4] if half == 1 else [y4, z], axis=1).reshape(2 * n * s, c)


def _dot_tn(a, b):
    return lax.dot_general(a, b, (((0,), (0,)), ((), ())), preferred_element_type=F32)


def _ada_kernel(c_ref, w_ref, b_ref, o_ref):
    s = _silu(c_ref[...]).astype(BF16)
    o_ref[0, 0] = _dot(s, w_ref[0].astype(BF16)) + b_ref[0, 0]


def ada_modulation(cvecs, w_ada, b_ada):
    depth = w_ada.shape[0]
    n = cvecs.shape[0]
    b4 = b_ada.reshape(depth, 6, 1, D_MODEL)
    return pl.pallas_call(
        _ada_kernel,
        out_shape=jax.ShapeDtypeStruct((depth, 6, n, D_MODEL), F32),
        grid=(depth, 6),
        in_specs=[pl.BlockSpec((n, D_MODEL), lambda l, j: (0, 0)),
                  pl.BlockSpec((1, D_MODEL, D_MODEL), lambda l, j: (l, 0, j)),
                  pl.BlockSpec((1, 1, 1, D_MODEL), lambda l, j: (l, j, 0, 0))],
        out_specs=pl.BlockSpec((1, 1, n, D_MODEL), lambda l, j: (l, j, 0, 0)),
        compiler_params=_cparams(("arbitrary", "arbitrary")),
        name="ada_modulation",
    )(cvecs, w_ada, b4)


def _in_kernel(x_ref, mod_ref, g_ref, w_ref, o_ref, *, row0, tiles_per_seq, n_chunk):
    row = row0 + pl.program_id(0) // tiles_per_seq
    sh = mod_ref[0, 0, pl.ds(row, 1), :]
    sc = mod_ref[0, 1, pl.ds(row, 1), :]
    h = _rms(x_ref[...], g_ref[...]) * (1.0 + sc) + sh
    hb = h.astype(BF16)
    for c in range(P_DIM // n_chunk):
        o_ref[:, c * n_chunk:(c + 1) * n_chunk] = _dot(hb, w_ref[:, c * n_chunk:(c + 1) * n_chunk])


def in_projection(x, mod, layer, g_pre, w_in, *, row0, per_seq, tm):
    t = x.shape[0]
    kern = functools.partial(_in_kernel, row0=row0, tiles_per_seq=per_seq // tm, n_chunk=512)
    return pl.pallas_call(
        kern,
        out_shape=jax.ShapeDtypeStruct((t, P_DIM), F32),
        grid=(t // tm,),
        in_specs=[pl.BlockSpec((tm, D_MODEL), lambda i: (i, 0)),
                  _const_spec((1, 6, mod.shape[2], D_MODEL), lambda i: (layer, 0, 0, 0)),
                  _const_spec((1, D_MODEL), lambda i: (0, 0)),
                  _const_spec((D_MODEL, P_DIM), lambda i: (0, 0))],
        out_specs=pl.BlockSpec((tm, P_DIM), lambda i: (i, 0)),
        compiler_params=_cparams(("parallel",), VMEM_LIMIT),
        name="in_projection",
    )(x, mod, g_pre, w_in)


CONF_HALO = 16
CONF_ROWS = 64


def _conf_kernel(av_ref, ag_ref, avp_ref, agp_ref, avn_ref, agn_ref, w_ref, b_ref, lg_ref, lb_ref,
                 o_ref, u_scr, rot_scr, *, tl, nt):
    t = pl.program_id(1)

    def glu(a_ref, g_ref):
        return a_ref[0] * jax.nn.sigmoid(g_ref[0])

    u_scr[0:CONF_HALO, :] = jnp.where(t > 0, glu(avp_ref, agp_ref), 0.0)
    u_scr[CONF_HALO:CONF_HALO + tl, :] = glu(av_ref, ag_ref)
    u_scr[CONF_HALO + tl:2 * CONF_HALO + tl, :] = jnp.where(t < nt - 1, glu(avn_ref, agn_ref), 0.0)
    n_ext = tl + 2 * CONF_HALO
    u_ext = u_scr[...]
    for r in range(1, 8):
        rot_scr[r - 1] = pltpu.roll(u_ext, n_ext - r, 0)
    half = CONV_K // 2
    for r0 in range(0, tl, CONF_ROWS):
        acc = jnp.zeros((CONF_ROWS, CONV_CH), F32) + b_ref[...]
        for k in range(CONV_K):
            off = r0 + k + CONF_HALO - half
            r = off % 8
            src = u_scr[off:off + CONF_ROWS, :] if r == 0 else rot_scr[r - 1, off - r:off - r + CONF_ROWS, :]
            acc = acc + w_ref[pl.ds(k, 1), :] * src
        mu = jnp.mean(acc, axis=-1, keepdims=True)
        cen = acc - mu
        var = jnp.mean(cen * cen, axis=-1, keepdims=True)
        y = cen * lax.rsqrt(var + EPS) * lg_ref[...] + lb_ref[...]
        o_ref[0, r0:r0 + CONF_ROWS, :] = _silu(y).astype(BF16)


def conformer_conv(p3, conv_w, conv_b, ln_g, ln_b, *, tl):
    b, l, _ = p3.shape
    nt = l // tl
    hb = tl // CONF_HALO
    last = l // CONF_HALO - 1
    kern = functools.partial(_conf_kernel, tl=tl, nt=nt)

    def main(col):
        return pl.BlockSpec((1, tl, CONV_CH), lambda i, t: (i, t, col))

    def prev(col):
        return pl.BlockSpec((1, CONF_HALO, CONV_CH), lambda i, t: (i, jnp.maximum(t * hb - 1, 0), col))

    def nxt(col):
        return pl.BlockSpec((1, CONF_HALO, CONV_CH), lambda i, t: (i, jnp.minimum((t + 1) * hb, last), col))

    cav, cag = COL_AV // CONV_CH, COL_AG // CONV_CH
    vec = lambda n: _const_spec((n, CONV_CH), lambda i, t: (0, 0))
    return pl.pallas_call(
        kern,
        out_shape=jax.ShapeDtypeStruct((b, l, CONV_CH), BF16),
        grid=(b, nt),
        in_specs=[main(cav), main(cag), prev(cav), prev(cag), nxt(cav), nxt(cag),
                  vec(CONV_K), vec(1), vec(1), vec(1)],
        out_specs=pl.BlockSpec((1, tl, CONV_CH), lambda i, t: (i, t, 0)),
        scratch_shapes=[pltpu.VMEM((tl + 2 * CONF_HALO, CONV_CH), F32),
                        pltpu.VMEM((7, tl + 2 * CONF_HALO, CONV_CH), F32)],
        compiler_params=_cparams(("parallel", "parallel")),
        name="conformer_conv",
    )(p3, p3, p3, p3, p3, p3, conv_w, conv_b, ln_g, ln_b)


GP_HALO = 8
BG_ROWS = 24


def _gprep_kernel(q_ref, k_ref, v_ref, qp_ref, kp_ref, vp_ref, qn_ref, kn_ref, vn_ref, bd_ref,
                  cw_ref, alog_ref, dtb_ref, qo_ref, ko_ref, vo_ref, bg_ref, bgt_ref, *, tl, nt):
    t = pl.program_id(1)
    gw = GDN_HEADS * GDN_D
    rows = lax.broadcasted_iota(jnp.int32, (tl, gw), 0)

    def conv_silu(x_ref, xp_ref, xn_ref, j):
        x = x_ref[0]
        prev_row = jnp.where(t > 0, xp_ref[0, GP_HALO - 1:GP_HALO, :], 0.0)
        next_row = jnp.where(t < nt - 1, xn_ref[0, 0:1, :], 0.0)
        up = jnp.where(rows == 0, prev_row, pltpu.roll(x, 1, 0))
        dn = jnp.where(rows == tl - 1, next_row, pltpu.roll(x, tl - 1, 0))
        w = cw_ref[:, j * gw:(j + 1) * gw]
        return _silu(up * w[0:1, :] + x * w[1:2, :] + dn * w[2:3, :])

    def l2n(seg, scale):
        return seg * (lax.rsqrt(jnp.sum(seg * seg, axis=-1, keepdims=True) + EPS) * scale)

    q = conv_silu(q_ref, qp_ref, qn_ref, 0)
    k = conv_silu(k_ref, kp_ref, kn_ref, 1)
    vo_ref[0] = conv_silu(v_ref, vp_ref, vn_ref, 2)
    for h in range(GDN_HEADS):
        sl = slice(h * GDN_D, (h + 1) * GDN_D)
        qo_ref[0, :, sl] = l2n(q[:, sl], GDN_D ** -0.5)
        ko_ref[0, :, sl] = l2n(k[:, sl], 1.0)

    bd = bd_ref[0]
    lane = lax.broadcasted_iota(jnp.int32, (tl, 128), 1)
    rc = lax.broadcasted_iota(jnp.int32, (tl, 128), 0) & (CHUNK - 1)
    is_g = (lane >= 8) & (lane < 16)
    beta = jax.nn.sigmoid(bd)
    xg = bd + dtb_ref[...]
    softplus = jnp.maximum(xg, 0.0) + jnp.log1p(jnp.exp(-jnp.abs(xg)))
    g = jnp.where(is_g, -jnp.exp(alog_ref[...]) * softplus, 0.0)
    pre = g
    suf = g
    s = 1
    while s < CHUNK:
        pre = pre + jnp.where(rc >= s, pltpu.roll(pre, s, 0), 0.0)
        suf = suf + jnp.where(rc < CHUNK - s, pltpu.roll(suf, tl - s, 0), 0.0)
        s *= 2
    tot = pre + suf - g
    bg = jnp.where(lane < 8, beta,
                   jnp.where(lane < 12, pre,
                             jnp.where(lane < 16, suf, pltpu.roll(tot, 8, 1))))
    bg = jnp.where(lane < 24, bg, 0.0)
    bg_ref[0] = bg
    bgt_ref[0] = jnp.transpose(bg)[0:BG_ROWS, :]


def gdn_prepare(p3, conv_w, alog_row, dtb_row, *, tl):
    b, l, _ = p3.shape
    nt = l // tl
    gw = GDN_HEADS * GDN_D
    hb = tl // GP_HALO
    last = l // GP_HALO - 1
    kern = functools.partial(_gprep_kernel, tl=tl, nt=nt)

    def main(col):
        return pl.BlockSpec((1, tl, gw), lambda i, t: (i, t, col))

    def prev(col):
        return pl.BlockSpec((1, GP_HALO, gw), lambda i, t: (i, jnp.maximum(t * hb - 1, 0), col))

    def nxt(col):
        return pl.BlockSpec((1, GP_HALO, gw), lambda i, t: (i, jnp.minimum((t + 1) * hb, last), col))

    cq, ck, cv = COL_Q // gw, COL_K // gw, COL_V // gw
    seq = jax.ShapeDtypeStruct((b, l, gw), F32)
    return pl.pallas_call(
        kern,
        out_shape=(seq, seq, seq, jax.ShapeDtypeStruct((b, l, 128), F32),
                   jax.ShapeDtypeStruct((b, BG_ROWS, l), F32)),
        grid=(b, nt),
        in_specs=[main(cq), main(ck), main(cv), prev(cq), prev(ck), prev(cv), nxt(cq), nxt(ck), nxt(cv),
                  pl.BlockSpec((1, tl, 128), lambda i, t: (i, t, COL_BD // 128)),
                  _const_spec((3, 3 * gw), lambda i, t: (0, 0)),
                  _const_spec((1, 128), lambda i, t: (0, 0)),
                  _const_spec((1, 128), lambda i, t: (0, 0))],
        out_specs=(main(0), main(0), main(0),
                   pl.BlockSpec((1, tl, 128), lambda i, t: (i, t, 0)),
                   pl.BlockSpec((1, BG_ROWS, tl), lambda i, t: (i, 0, t))),
        compiler_params=_cparams(("parallel", "parallel")),
        name="gdn_prepare",
    )(p3, p3, p3, p3, p3, p3, p3, p3, p3, p3, conv_w, alog_row, dtb_row)


def _gscan_kernel(qf_ref, kf_ref, vf_ref, bgf_ref, bgtf_ref, qb_ref, kb_ref, vb_ref, bgb_ref, bgtb_ref,
                  s0_ref, of_ref, ob_ref, sfin_ref, s_scr, vn_scr, *, r):
    nb = pl.program_id(1)
    nchunk = r // CHUNK
    shift = int(math.log2(CHUNK))

    @pl.when(nb == 0)
    def _():
        s_scr[...] = s0_ref[0, 0]

    vn_scr[...] = jnp.zeros((2 * GDN_HEADS, r, GDN_D), F32)
    ii = lax.broadcasted_iota(jnp.int32, (r, r), 0)
    jj = lax.broadcasted_iota(jnp.int32, (r, r), 1)
    blk_same = {s: (ii >> s) == (jj >> s) for s in range(3, shift + 1)}
    same = blk_same[shift]
    eye = (ii == jj).astype(F32)
    dirs = ((qf_ref, kf_ref, vf_ref, bgf_ref, bgtf_ref, of_ref), (qb_ref, kb_ref, vb_ref, bgb_ref, bgtb_ref, ob_ref))
    masks = ((same & (ii >= jj), same & (ii > jj)), (same & (ii <= jj), same & (ii < jj)))
    blocks = [(d, h) for d in range(2) for h in range(GDN_HEADS)]
    nblk = len(blocks)

    def each(fn):
        return [fn(i, *blocks[i]) for i in range(nblk)]

    def col(off):
        return each(lambda i, d, h: dirs[d][3][0, :, off + d * GDN_HEADS + h:off + d * GDN_HEADS + h + 1])

    sl_h = [slice(h * GDN_D, (h + 1) * GDN_D) for h in range(GDN_HEADS)]
    beta, gcc, gl = col(0), col(8), col(16)
    gcr = each(lambda i, d, h: dirs[d][4][0, 8 + d * GDN_HEADS + h:9 + d * GDN_HEADS + h, :])
    q = each(lambda i, d, h: dirs[d][0][0, :, sl_h[h]])
    k = each(lambda i, d, h: dirs[d][1][0, :, sl_h[h]])
    v = each(lambda i, d, h: dirs[d][2][0, :, sl_h[h]])
    decay = each(lambda i, d, h: jnp.exp(jnp.where(masks[d][0], gcc[i] - gcr[i], NEG)))
    kbeta = each(lambda i, d, h: k[i] * beta[i])
    kb16 = each(lambda i, d, h: k[i].astype(BF16))
    lm = each(lambda i, d, h: jnp.where(masks[d][1], _dot_nt(kbeta[i].astype(BF16), kb16[i]) * decay[i], 0.0))
    egc = each(lambda i, d, h: jnp.exp(gcc[i]))
    x = each(lambda i, d, h: jnp.concatenate([v[i] * beta[i], kbeta[i] * egc[i]], axis=1))
    l8 = each(lambda i, d, h: jnp.where(blk_same[3], lm[i], 0.0))
    m2 = each(lambda i, d, h: _mm(l8[i], l8[i]))
    m4 = each(lambda i, d, h: _mm(m2[i], m2[i]))
    t = each(lambda i, d, h: eye - l8[i])
    t = each(lambda i, d, h: t[i] + _mm(t[i], m2[i]))
    t = each(lambda i, d, h: t[i] + _mm(t[i], m4[i]))
    for lvl in range(3, shift):
        s_rows = 1 << lvl
        off = blk_same[lvl + 1] & jnp.logical_not(blk_same[lvl])
        ct = each(lambda i, d, h: _mm(_half_rows(jnp.where(off, lm[i], 0.0), s_rows, 1 - d), t[i]))
        upd = each(lambda i, d, h: _mm(_half_rows(t[i], s_rows, 1 - d), _expand_rows(ct[i], s_rows, 1 - d)))
        t = each(lambda i, d, h: t[i] - _expand_rows(upd[i], s_rows, 1 - d))
    x = each(lambda i, d, h: _mm(t[i], x[i]))
    intra = each(lambda i, d, h: (_dot_nt(q[i].astype(BF16), kb16[i]) * decay[i]).astype(BF16))
    qd = each(lambda i, d, h: q[i] * egc[i])
    kd = each(lambda i, d, h: (k[i] * jnp.exp(gl[i] - gcc[i])).astype(BF16))
    eg_last = each(lambda i, d, h: jnp.exp(jnp.broadcast_to(gl[i], (r, GDN_D))))
    state = each(lambda i, d, h: s_scr[d, h])
    outs = [[None] * nchunk for _ in range(nblk)]
    for step in range(nchunk):
        cs = each(lambda i, d, h: step if d == 0 else nchunk - 1 - step)
        sls = [slice(c * CHUNK, (c + 1) * CHUNK) for c in cs]
        a = each(lambda i, d, h: _dot(jnp.concatenate([x[i][sls[i], GDN_D:], qd[i][sls[i]]], axis=0).astype(BF16),
                                      state[i].astype(BF16)))
        v_new = each(lambda i, d, h: x[i][sls[i], :GDN_D] - a[i][:CHUNK])
        for i in range(nblk):
            vn_scr[i, sls[i], :] = v_new[i]
        for i in range(nblk):
            outs[i][cs[i]] = a[i][CHUNK:] + _dot(intra[i][sls[i], :], vn_scr[i].astype(BF16))
        state = each(lambda i, d, h: state[i] * eg_last[i][cs[i] * CHUNK:cs[i] * CHUNK + 1, :]
                     + _dot_tn(kd[i][sls[i]], v_new[i].astype(BF16)))
    for i, (d, h) in enumerate(blocks):
        s_scr[d, h] = state[i]
        dirs[d][5][0, :, sl_h[h]] = jnp.concatenate(outs[i], axis=0)
    sfin_ref[0] = s_scr[...]


def gdn_scan(q, k, v, bg, bgt, state0, layer):
    b, l, gw = q.shape
    r = min(GDN_BLOCK, l)
    nbk = l // r
    kern = functools.partial(_gscan_kernel, r=r)
    fwd = lambda i, n: (i, n, 0)
    bwd = lambda i, n: (i, nbk - 1 - n, 0)
    fwd_t = lambda i, n: (i, 0, n)
    bwd_t = lambda i, n: (i, 0, nbk - 1 - n)
    seq = lambda im: pl.BlockSpec((1, r, gw), im)
    st_shape = (2, GDN_HEADS, GDN_D, GDN_D)
    return pl.pallas_call(
        kern,
        out_shape=(jax.ShapeDtypeStruct((b, l, gw), F32), jax.ShapeDtypeStruct((b, l, gw), F32),
                   jax.ShapeDtypeStruct((b,) + st_shape, F32)),
        grid=(b, nbk),
        in_specs=[seq(fwd), seq(fwd), seq(fwd), pl.BlockSpec((1, r, 128), fwd), pl.BlockSpec((1, BG_ROWS, r), fwd_t),
                  seq(bwd), seq(bwd), seq(bwd), pl.BlockSpec((1, r, 128), bwd), pl.BlockSpec((1, BG_ROWS, r), bwd_t),
                  pl.BlockSpec((1, 1) + st_shape, lambda i, n: (i, layer, 0, 0, 0, 0))],
        out_specs=(seq(fwd), seq(bwd), pl.BlockSpec((1,) + st_shape, lambda i, n: (i, 0, 0, 0, 0))),
        scratch_shapes=[pltpu.VMEM(st_shape, F32), pltpu.VMEM((2 * GDN_HEADS, r, GDN_D), F32)],
        compiler_params=_cparams(("parallel", "arbitrary"), VMEM_LIMIT),
        name="gdn_scan",
    )(q, k, v, bg, bgt, q, k, v, bg, bgt, state0)


def _softmax_pv(q, parts):
    heads = range(NA_HEADS)
    lane = lax.broadcasted_iota(jnp.int32, (1, NA_HEADS * NA_DIM), 1)
    hm = [(lane >= h * NA_DIM) & (lane < (h + 1) * NA_DIM) for h in heads]
    qh = [jnp.where(hm[h], q, 0.0).astype(BF16) for h in heads]
    s = [[_dot_nt(qh[h], keys) if bias is None else _dot_nt(qh[h], keys) + bias(h) for keys, _, bias in parts]
         for h in heads]
    mx = [functools.reduce(jnp.maximum, [jnp.max(sp, axis=-1, keepdims=True) for sp in s[h]]) for h in heads]
    e = [[jnp.exp(sp - mx[h]) for sp in s[h]] for h in heads]
    inv = [1.0 / sum(jnp.sum(ep, axis=-1, keepdims=True) for ep in e[h]) for h in heads]
    o = [sum(_dot(ep.astype(BF16), vals) for ep, (_, vals, _) in zip(e[h], parts)) * inv[h] for h in heads]
    out = o[NA_HEADS - 1]
    for h in range(NA_HEADS - 2, -1, -1):
        out = jnp.where(hm[h], o[h], out)
    return out


def _attn_ctx_kernel(q_ref, k_ref, v_ref, o_ref):
    q = q_ref[0] * (NA_DIM ** -0.5)
    o_ref[0] = _softmax_pv(q, [(k_ref[0].astype(BF16), v_ref[0].astype(BF16), None)]).astype(BF16)


def context_attention(p3):
    b, l, _ = p3.shape
    w = NA_HEADS * NA_DIM
    col = lambda c: pl.BlockSpec((1, l, w), lambda i: (i, 0, c // w))
    return pl.pallas_call(
        _attn_ctx_kernel,
        out_shape=jax.ShapeDtypeStruct((b, l, w), BF16),
        grid=(b,),
        in_specs=[col(COL_QC), col(COL_KC), col(COL_VC)],
        out_specs=pl.BlockSpec((1, l, w), lambda i: (i, 0, 0)),
        compiler_params=_cparams(("parallel",)),
        name="context_attention",
    )(p3, p3, p3)


def _attn_lat_kernel(q_ref, k_ref, v_ref, kc_ref, vc_ref, bias_ref, o_ref, *, rows):
    i = pl.program_id(1)
    nwin = NA_WROWS * GRID_W
    ws = jnp.clip(i * NA_QROWS - WIN_H // 2, 0, rows - NA_WROWS)
    start = pl.multiple_of(ws * GRID_W, GRID_W)
    q = q_ref[0] * (NA_DIM ** -0.5)
    window = (k_ref[0, pl.ds(start, nwin), :].astype(BF16), v_ref[0, pl.ds(start, nwin), :].astype(BF16),
              lambda h: bias_ref[0, h])
    context = (kc_ref[0].astype(BF16), vc_ref[0].astype(BF16), None)
    o_ref[0] = _softmax_pv(q, [window, context]).astype(BF16)


def neighbourhood_attention(p3, k_ctx, v_ctx, bias_tab):
    b, n, _ = p3.shape
    rows = n // GRID_W
    w = NA_HEADS * NA_DIM
    tq = NA_QROWS * GRID_W
    ni = rows // NA_QROWS
    nctx = k_ctx.shape[1]
    kern = functools.partial(_attn_lat_kernel, rows=rows)
    case = lambda i, t: (jnp.where(t == 0, 0, jnp.where(t == ni - 1, 2, 1)), 0, 0, 0)
    return pl.pallas_call(
        kern,
        out_shape=jax.ShapeDtypeStruct((b, n, w), BF16),
        grid=(b, ni),
        in_specs=[pl.BlockSpec((1, tq, w), lambda i, t: (i, t, COL_QC // w)),
                  pl.BlockSpec((1, n, w), lambda i, t: (i, 0, COL_KC // w)),
                  pl.BlockSpec((1, n, w), lambda i, t: (i, 0, COL_VC // w)),
                  pl.BlockSpec((1, nctx, w), lambda i, t: (i, 0, 0)),
                  pl.BlockSpec((1, nctx, w), lambda i, t: (i, 0, 0)),
                  pl.BlockSpec((1, NA_HEADS, tq, NA_WROWS * GRID_W), case)],
        out_specs=pl.BlockSpec((1, tq, w), lambda i, t: (i, t, 0)),
        compiler_params=_cparams(("parallel", "arbitrary"), VMEM_LIMIT),
        name="neighbourhood_attention",
    )(p3, p3, p3, k_ctx, v_ctx, bias_tab)


def _bias_kernel(rpb_ref, o_ref, *, rows):
    l = pl.program_id(0)
    h = pl.program_id(1)
    n_dr, n_dc = 2 * WIN_H - 1, 2 * WIN_W - 1
    base = (l * NA_HEADS + h) * (n_dr * n_dc)
    c = lax.broadcasted_iota(jnp.int32, (GRID_W, 2 * GRID_W), 0)
    lane = lax.broadcasted_iota(jnp.int32, (GRID_W, 2 * GRID_W), 1)
    kc = lane & (GRID_W - 1)
    dc = jnp.clip(kc - c + (WIN_W - 1), 0, n_dc - 1)
    cs = jnp.clip(c - WIN_W // 2, 0, GRID_W - WIN_W)
    col_ok = (kc >= cs) & (kc < cs + WIN_W)
    neg = jnp.full((GRID_W, 2 * GRID_W), NEG, F32)
    tabs = []
    for a in range(n_dr):
        t = jnp.zeros((GRID_W, 2 * GRID_W), F32)
        for bb in range(n_dc):
            t = jnp.where(dc == bb, rpb_ref[base + a * n_dc + bb], t)
        tabs.append(jnp.where(col_ok, t, NEG))
    interior = 2 * NA_QROWS
    cases = ((0, 0), (interior, interior - WIN_H // 2), (rows - NA_QROWS, rows - NA_WROWS))
    for case, (r0, ws) in enumerate(cases):
        for iq in range(NA_QROWS):
            r = r0 + iq
            rs = min(max(r - WIN_H // 2, 0), rows - WIN_H)
            for jp in range(NA_WROWS // 2):
                halves = []
                for half in range(2):
                    kr = ws + 2 * jp + half
                    halves.append(tabs[kr - r + WIN_H - 1] if rs <= kr < rs + WIN_H else neg)
                blk = jnp.where(lane < GRID_W, halves[0], halves[1])
                o_ref[0, case, 0, iq * GRID_W:(iq + 1) * GRID_W, jp * 2 * GRID_W:(jp + 1) * 2 * GRID_W] = blk


def attention_bias_tables(na_rpb, rows):
    depth = na_rpb.shape[0]
    kern = functools.partial(_bias_kernel, rows=rows)
    tq, nw = NA_QROWS * GRID_W, NA_WROWS * GRID_W
    return pl.pallas_call(
        kern,
        out_shape=jax.ShapeDtypeStruct((depth, 3, NA_HEADS, tq, nw), F32),
        grid=(depth, NA_HEADS),
        in_specs=[pl.BlockSpec(memory_space=pltpu.SMEM)],
        out_specs=pl.BlockSpec((1, 3, 1, tq, nw), lambda l, h: (l, 0, h, 0, 0)),
        compiler_params=_cparams(("arbitrary", "arbitrary")),
        name="attention_bias_tables",
    )(na_rpb.reshape(-1))


def _out_kernel(x_ref, oa_ref, of_ref, ob_ref, z_ref, oc_ref, w_ref, mod_ref, gpost_ref, gpre_ref, gnw_ref,
                x1_ref, h2_ref, *, row0, tiles_per_seq):
    row = row0 + pl.program_id(0) // tiles_per_seq
    g1 = mod_ref[0, 2, pl.ds(row, 1), :]
    sh2 = mod_ref[0, 3, pl.ds(row, 1), :]
    sc2 = mod_ref[0, 4, pl.ds(row, 1), :]
    m = _dot(oa_ref[...], w_ref[0:CONV_CH, :])
    og = of_ref[...] + ob_ref[...]
    z = z_ref[...]
    for h in range(GDN_HEADS):
        sl = slice(h * GDN_D, (h + 1) * GDN_D)
        gated = _rms(og[:, sl], gnw_ref[...]) * _silu(z[:, sl])
        m = m + _dot(gated.astype(BF16), w_ref[CONV_CH + h * GDN_D:CONV_CH + (h + 1) * GDN_D, :])
    m = m + _dot(oc_ref[...], w_ref[CONV_CH + GDN_HEADS * GDN_D:, :])
    x1 = x_ref[...] + g1 * _rms(m, gpost_ref[...])
    x1_ref[...] = x1
    h2_ref[...] = (_rms(x1, gpre_ref[...]) * (1.0 + sc2) + sh2).astype(BF16)


def out_projection(x, oa, of, ob, p, oc, w_out, mod, layer, g_post, g_pre_ffn, gnw, *, row0, per_seq, tm):
    t = x.shape[0]
    gw = GDN_HEADS * GDN_D
    kern = functools.partial(_out_kernel, row0=row0, tiles_per_seq=per_seq // tm)
    rowblk = lambda n: pl.BlockSpec((tm, n), lambda i: (i, 0))
    vec = lambda n: _const_spec((1, n), lambda i: (0, 0))
    return pl.pallas_call(
        kern,
        out_shape=(jax.ShapeDtypeStruct((t, D_MODEL), F32), jax.ShapeDtypeStruct((t, D_MODEL), BF16)),
        grid=(t // tm,),
        in_specs=[rowblk(D_MODEL), rowblk(CONV_CH), rowblk(gw), rowblk(gw),
                  pl.BlockSpec((tm, gw), lambda i: (i, COL_Z // gw)), rowblk(NA_HEADS * NA_DIM),
                  _const_spec((D_MODEL, D_MODEL), lambda i: (0, 0)),
                  _const_spec((1, 6, mod.shape[2], D_MODEL), lambda i: (layer, 0, 0, 0)),
                  vec(D_MODEL), vec(D_MODEL), vec(GDN_D)],
        out_specs=(rowblk(D_MODEL), rowblk(D_MODEL)),
        compiler_params=_cparams(("parallel",), VMEM_LIMIT),
        name="out_projection",
    )(x, oa, of, ob, p, oc, w_out, mod, g_post, g_pre_ffn, gnw)


FFN_HALO = 16
FFN_CHUNK = 256


def _ffn_kernel(h_ref, hp_ref, hn_ref, x_ref, wg_ref, wv_ref, wd_ref, cwg_ref, cwv_ref, cbg_ref, cbv_ref,
                mod_ref, gpost_ref, o_ref, hext_scr, act_scr, *, row0, tiles_per_mod, tiles_per_seq, tm):
    i = pl.program_id(0)
    row = row0 + i // tiles_per_mod
    g2 = mod_ref[0, 5, pl.ds(row, 1), :]
    hext_scr[0:tm, :] = h_ref[...]
    hext_scr[tm:tm + FFN_HALO, :] = hp_ref[...]
    hext_scr[tm + FFN_HALO:tm + 2 * FFN_HALO, :] = hn_ref[...]
    t_in_seq = i % tiles_per_seq
    keep_prev = (t_in_seq > 0).astype(F32)
    keep_next = (t_in_seq < tiles_per_seq - 1).astype(F32)
    row8 = lax.broadcasted_iota(jnp.int32, (8, FFN_CHUNK), 0)

    def conv(u_ext, cw_ref, cb_ref, sl):
        u = u_ext[0:tm]
        prev_row = u_ext[tm + FFN_HALO - 1:tm + FFN_HALO] * keep_prev
        next_row = u_ext[tm + FFN_HALO:tm + FFN_HALO + 1] * keep_next
        up = pltpu.roll(u, 1, 0)
        dn = pltpu.roll(u, tm - 1, 0)
        up = jnp.concatenate([jnp.where(row8 == 0, prev_row, up[0:8]), up[8:]], axis=0)
        dn = jnp.concatenate([dn[:tm - 8], jnp.where(row8 == 7, next_row, dn[tm - 8:])], axis=0)
        return up * cw_ref[0:1, sl] + u * cw_ref[1:2, sl] + dn * cw_ref[2:3, sl] + cb_ref[:, sl]

    n_chunks = D_FF // FFN_CHUNK
    sls = [slice(c * FFN_CHUNK, (c + 1) * FFN_CHUNK) for c in range(n_chunks)]

    def up_proj(c):
        return _dot(hext_scr[...], wg_ref[:, sls[c]]), _dot(hext_scr[...], wv_ref[:, sls[c]])

    nxt = up_proj(0)
    for c in range(n_chunks):
        ug, uv = nxt
        if c + 1 < n_chunks:
            nxt = up_proj(c + 1)
        gate = conv(ug, cwg_ref, cbg_ref, sls[c])
        val = conv(uv, cwv_ref, cbv_ref, sls[c])
        act_scr[:, sls[c]] = (_silu(gate) * val).astype(BF16)
    f = _dot(act_scr[...], wd_ref[...])
    o_ref[...] = x_ref[...] + g2 * _rms(f, gpost_ref[...])


def conv_ffn(h2, x1, wg, wv, wd, cwg, cwv, cbg, cbv, mod, layer, g_post, *, row0, per_seq, tm, seq_len):
    t = x1.shape[0]
    hb = tm // FFN_HALO
    last = t // FFN_HALO - 1
    assert seq_len % tm == 0 and per_seq % seq_len == 0
    kern = functools.partial(_ffn_kernel, row0=row0, tiles_per_mod=per_seq // tm, tiles_per_seq=seq_len // tm, tm=tm)
    rowblk = lambda n: pl.BlockSpec((tm, n), lambda i: (i, 0))
    return pl.pallas_call(
        kern,
        out_shape=jax.ShapeDtypeStruct((t, D_MODEL), F32),
        grid=(t // tm,),
        in_specs=[rowblk(D_MODEL),
                  pl.BlockSpec((FFN_HALO, D_MODEL), lambda i: (jnp.maximum(i * hb - 1, 0), 0)),
                  pl.BlockSpec((FFN_HALO, D_MODEL), lambda i: (jnp.minimum((i + 1) * hb, last), 0)),
                  rowblk(D_MODEL),
                  _const_spec((D_MODEL, D_FF), lambda i: (0, 0)),
                  _const_spec((D_MODEL, D_FF), lambda i: (0, 0)),
                  _const_spec((D_FF, D_MODEL), lambda i: (0, 0)),
                  _const_spec((3, D_FF), lambda i: (0, 0)),
                  _const_spec((3, D_FF), lambda i: (0, 0)),
                  _const_spec((1, D_FF), lambda i: (0, 0)),
                  _const_spec((1, D_FF), lambda i: (0, 0)),
                  _const_spec((1, 6, mod.shape[2], D_MODEL), lambda i: (layer, 0, 0, 0)),
                  _const_spec((1, D_MODEL), lambda i: (0, 0))],
        out_specs=rowblk(D_MODEL),
        scratch_shapes=[pltpu.VMEM((tm + 2 * FFN_HALO, D_MODEL), BF16), pltpu.VMEM((tm, D_FF), BF16)],
        compiler_params=_cparams(("parallel",), VMEM_LIMIT),
        name="conv_ffn",
    )(h2, h2, h2, x1, wg, wv, wd, cwg, cwv, cbg, cbv, mod, g_post)


def _trunk_layer(x, lw, layer, mod, *, batch, seq, row0, per_seq, ctx):
    tm = min(512, seq)
    p = in_projection(x, mod, layer, lw["g_pre_mix"], lw["w_in"], row0=row0, per_seq=per_seq, tm=min(256, seq))
    p3 = p.reshape(batch, seq, P_DIM)
    oa = conformer_conv(p3, lw["conv_w"], lw["conv_b"], lw["conv_ln_g"], lw["conv_ln_b"], tl=min(512, seq))
    qn, kn, vn, bg, bgt = gdn_prepare(p3, lw["gdn_conv_w"], lw["alog_row"], lw["dtb_row"], tl=min(256, seq))
    if ctx is None:
        state0 = jnp.zeros((batch, 1, 2, GDN_HEADS, GDN_D, GDN_D), F32)
        of, ob, s_fin = gdn_scan(qn, kn, vn, bg, bgt, state0, 0)
        oc = context_attention(p3)
    else:
        k_ctx, v_ctx, bias_tab, state_delta = ctx
        of, ob, s_fin = gdn_scan(qn, kn, vn, bg, bgt, state_delta, layer)
        oc = neighbourhood_attention(p3, k_ctx, v_ctx, bias_tab)
    t = batch * seq
    gw = GDN_HEADS * GDN_D
    x1, h2 = out_projection(x, oa.reshape(t, CONV_CH), of.reshape(t, gw), ob.reshape(t, gw), p,
                            oc.reshape(t, NA_HEADS * NA_DIM), lw["w_out"], mod, layer, lw["g_post_mix"],
                            lw["g_pre_ffn"], lw["gdn_norm_w"], row0=row0, per_seq=per_seq, tm=tm)
    x2 = conv_ffn(h2, x1, lw["wg"], lw["wv"], lw["wd"], lw["cwg"], lw["cwv"], lw["cbg"], lw["cbv"], mod, layer,
                  lw["g_post_ffn"], row0=row0, per_seq=per_seq, tm=tm, seq_len=seq)
    return x2, p3, s_fin


def kernel(x_prompt, x_sample, cache_attn_kv, state_delta, c, c_ctx, w_ada, b_ada, g_pre_mix, g_post_mix, g_pre_ffn, g_post_ffn, w_in, w_out, conv_w, conv_b, conv_ln_g, conv_ln_b, gdn_conv_w, gdn_a_log, gdn_dt_bias, gdn_norm_w, na_rpb, w_up, ffn_conv_w, ffn_conv_b, w_down):
    depth = w_in.shape[0]
    bp, sp, d = x_prompt.shape
    bs, ss, _ = x_sample.shape
    rows = ss // GRID_W
    nmod = 16
    assert 1 + bs <= nmod

    n_gdn = 2 * GDN_HEADS
    c0 = 2 * CONV_CH + 4 * GDN_HEADS * GDN_D
    w_in_r = jnp.concatenate(
        [w_in[:, :, :c0], w_in[:, :, c0 + 2 * n_gdn:], w_in[:, :, c0:c0 + 2 * n_gdn],
         jnp.zeros((depth, d, P_DIM - w_in.shape[2]), w_in.dtype)], axis=-1).astype(BF16)
    w_out_b = w_out.astype(BF16)
    wg_b = w_up[:, :, :D_FF].astype(BF16)
    wv_b = w_up[:, :, D_FF:].astype(BF16)
    wd_b = w_down.astype(BF16)

    def lane_row(vals):
        return jnp.pad(vals.reshape(depth, 1, n_gdn), ((0, 0), (0, 0), (8, 128 - 8 - n_gdn)))

    alog_rows = lane_row(gdn_a_log)
    dtb_rows = lane_row(gdn_dt_bias)

    cvecs = jnp.concatenate([c_ctx[None, :], c, jnp.zeros((nmod - 1 - bs, d), F32)], axis=0)
    mod = ada_modulation(cvecs, w_ada, b_ada)
    bias_tabs = attention_bias_tables(na_rpb, rows)

    xp = x_prompt.reshape(bp * sp, d)
    xs = x_sample.reshape(bs * ss, d)
    kv_new, st_new = [], []
    hw = NA_HEADS * NA_DIM
    for l in range(depth):
        lw = {
            "g_pre_mix": g_pre_mix[l][None], "g_post_mix": g_post_mix[l][None],
            "g_pre_ffn": g_pre_ffn[l][None], "g_post_ffn": g_post_ffn[l][None],
            "w_in": w_in_r[l], "w_out": w_out_b[l],
            "conv_w": conv_w[l], "conv_b": conv_b[l][None], "conv_ln_g": conv_ln_g[l][None],
            "conv_ln_b": conv_ln_b[l][None],
            "gdn_conv_w": gdn_conv_w[l], "alog_row": alog_rows[l], "dtb_row": dtb_rows[l],
            "gdn_norm_w": gdn_norm_w[l][None],
            "wg": wg_b[l], "wv": wv_b[l], "wd": wd_b[l],
            "cwg": ffn_conv_w[l][:, :D_FF], "cwv": ffn_conv_w[l][:, D_FF:],
            "cbg": ffn_conv_b[l][None, :D_FF], "cbv": ffn_conv_b[l][None, D_FF:],
        }
        xp, p3, s_fin = _trunk_layer(xp, lw, l, mod, batch=bp, seq=sp, row0=0, per_seq=bp * sp, ctx=None)
        heads = lambda col: p3[:, :, col:col + hw].reshape(bp, sp, NA_HEADS, NA_DIM).transpose(0, 2, 1, 3)
        kv_new.append(jnp.stack([heads(COL_KC), heads(COL_VC)], axis=1))
        st_new.append(s_fin)
        lanes = lambda t: t.transpose(0, 2, 1, 3).reshape(bs, t.shape[2], hw)
        ctx = (lanes(cache_attn_kv[:, l, 0]), lanes(cache_attn_kv[:, l, 1]), bias_tabs[l], state_delta)
        xs, _, _ = _trunk_layer(xs, lw, l, mod, batch=bs, seq=ss, row0=1, per_seq=ss, ctx=ctx)
    return (xp.reshape(bp, sp, d), xs.reshape(bs, ss, d), jnp.stack(kv_new, axis=1), jnp.stack(st_new, axis=1))
```

```python
import functools
import math

import jax
import jax.numpy as jnp
from jax import lax
from jax.experimental import pallas as pl
from jax.experimental.pallas import tpu as pltpu

F32 = jnp.float32
BF16 = jnp.bfloat16
EPS = 1e-6
NEG = -1e30

D_MODEL = 1024
CONV_CH = 256
CONV_K = 31
GDN_HEADS = 4
GDN_D = 128
NA_HEADS = 4
NA_DIM = 64
GRID_W = 64
WIN_H = 8
WIN_W = 16
D_FF = 2816
CHUNK = 64
GDN_BLOCK = 256
NA_QROWS = 4
NA_WROWS = 12

COL_AV, COL_AG = 0, 256
COL_Q, COL_K, COL_V, COL_Z = 512, 1024, 1536, 2048
COL_QC, COL_KC, COL_VC = 2560, 2816, 3072
COL_BD = 3328
P_DIM = 3456

VMEM_LIMIT = 56 * 1024 * 1024


def _cparams(sem, vmem=None):
    return pltpu.CompilerParams(dimension_semantics=sem, vmem_limit_bytes=vmem)


def _const_spec(shape, index_map):
    return pl.BlockSpec(shape, index_map, pipeline_mode=pl.Buffered(1))


def _silu(x):
    return x * jax.nn.sigmoid(x)


def _rms(x, g):
    return x * lax.rsqrt(jnp.mean(x * x, axis=-1, keepdims=True) + EPS) * g


def _dot(a, b):
    return jnp.dot(a, b, preferred_element_type=F32)


def _dot_nt(a, b):
    return lax.dot_general(a, b, (((1,), (1,)), ((), ())), preferred_element_type=F32)


def _dot_tn(a, b):
    return lax.dot_general(a, b, (((0,), (0,)), ((), ())), preferred_element_type=F32)


def _mm(a, b):
    return _dot(a.astype(BF16), b.astype(BF16))


def _half_rows(x, s, half):
    n, c = x.shape[0] // (2 * s), x.shape[1]
    return x.reshape(n, 2, s, c)[:, half].reshape(n * s, c)


def _expand_rows(y, s, half):
    n, c = y.shape[0] // s, y.shape[1]
    y4 = y.reshape(n, 1, s, c)
    z = jnp.zeros_like(y4)
    return jnp.concatenate([z, y4] if half == 1 else [y4, z], axis=1).reshape(2 * n * s, c)


def _ada_kernel(c_ref, w_ref, b_ref, o_ref):
    s = _silu(c_ref[...]).astype(BF16)
    o_ref[0, 0] = _dot(s, w_ref[0].astype(BF16)) + b_ref[0, 0]


def ada_modulation(cvecs, w_ada, b_ada):
    depth = w_ada.shape[0]
    n = cvecs.shape[0]
    b4 = b_ada.reshape(depth, 6, 1, D_MODEL)
    return pl.pallas_call(
        _ada_kernel,
        out_shape=jax.ShapeDtypeStruct((depth, 6, n, D_MODEL), F32),
        grid=(depth, 6),
        in_specs=[pl.BlockSpec((n, D_MODEL), lambda l, j: (0, 0)),
                  pl.BlockSpec((1, D_MODEL, D_MODEL), lambda l, j: (l, 0, j)),
                  pl.BlockSpec((1, 1, 1, D_MODEL), lambda l, j: (l, j, 0, 0))],
        out_specs=pl.BlockSpec((1, 1, n, D_MODEL), lambda l, j: (l, j, 0, 0)),
        compiler_params=_cparams(("arbitrary", "arbitrary")),
        name="ada_modulation",
    )(cvecs, w_ada, b4)


MIX_HALO = 16
CONF_ROWS = 64
BG_ROWS = 24
REST_COLS = COL_BD - COL_Z
REST_Z, REST_QC, REST_KC, REST_VC = 0, COL_QC - COL_Z, COL_KC - COL_Z, COL_VC - COL_Z


def _mix_in_kernel(x_ref, xp_ref, xn_ref, mod_ref, g_ref, w_ref, cw_ref, cb_ref, lg_ref, lb_ref, gcw_ref,
                   alog_ref, dtb_ref, rest_ref, oa_ref, qo_ref, ko_ref, vo_ref, bg_ref, bgt_ref,
                   hext_scr, u_scr, rot_scr, *, row0, tiles_per_mod, tiles_per_seq, tm):
    i = pl.program_id(0)
    row = row0 + i // tiles_per_mod
    t_in_seq = i % tiles_per_seq
    keep_prev = (t_in_seq > 0).astype(F32)
    keep_next = (t_in_seq < tiles_per_seq - 1).astype(F32)
    sh = mod_ref[0, 0, pl.ds(row, 1), :]
    sc = mod_ref[0, 1, pl.ds(row, 1), :]
    hh = MIX_HALO
    gw = GDN_HEADS * GDN_D

    def norm(x):
        return (_rms(x, g_ref[...]) * (1.0 + sc) + sh).astype(BF16)

    hext_scr[0:hh, :] = norm(xp_ref[...])
    hext_scr[hh:hh + tm, :] = norm(x_ref[...])
    hext_scr[hh + tm:2 * hh + tm, :] = norm(xn_ref[...])

    def proj_ext(col, width):
        return _dot(hext_scr[...], w_ref[:, col:col + width])

    def proj(col, width):
        return _dot(hext_scr[hh:hh + tm, :], w_ref[:, col:col + width])

    pa = proj_ext(COL_AV, 2 * CONV_CH)
    pq = proj_ext(COL_Q, gw)
    u = pa[:, :CONV_CH] * jax.nn.sigmoid(pa[:, CONV_CH:])
    u_scr[0:hh, :] = u[0:hh] * keep_prev
    u_scr[hh:hh + tm, :] = u[hh:hh + tm]
    u_scr[hh + tm:2 * hh + tm, :] = u[hh + tm:] * keep_next
    n_ext = tm + 2 * hh
    u_ext = u_scr[...]
    for r in range(1, 8):
        rot_scr[r - 1] = pltpu.roll(u_ext, n_ext - r, 0)
    pk = proj_ext(COL_K, gw)
    half = CONV_K // 2
    for r0 in range(0, tm, CONF_ROWS):
        acc = jnp.zeros((CONF_ROWS, CONV_CH), F32) + cb_ref[...]
        for k in range(CONV_K):
            off = r0 + k + hh - half
            r = off % 8
            src = u_scr[off:off + CONF_ROWS, :] if r == 0 else rot_scr[r - 1, off - r:off - r + CONF_ROWS, :]
            acc = acc + cw_ref[pl.ds(k, 1), :] * src
        mu = jnp.mean(acc, axis=-1, keepdims=True)
        cen = acc - mu
        var = jnp.mean(cen * cen, axis=-1, keepdims=True)
        y = cen * lax.rsqrt(var + EPS) * lg_ref[...] + lb_ref[...]
        oa_ref[r0:r0 + CONF_ROWS, :] = _silu(y).astype(BF16)

    row8 = lax.broadcasted_iota(jnp.int32, (8, gw), 0)

    def conv_silu(pe, j):
        x = pe[hh:hh + tm]
        prev_row = pe[hh - 1:hh] * keep_prev
        next_row = pe[hh + tm:hh + tm + 1] * keep_next
        up = pltpu.roll(x, 1, 0)
        dn = pltpu.roll(x, tm - 1, 0)
        up = jnp.concatenate([jnp.where(row8 == 0, prev_row, up[0:8]), up[8:]], axis=0)
        dn = jnp.concatenate([dn[:tm - 8], jnp.where(row8 == 7, next_row, dn[tm - 8:])], axis=0)
        w = gcw_ref[:, j * gw:(j + 1) * gw]
        return _silu(up * w[0:1, :] + x * w[1:2, :] + dn * w[2:3, :])

    def l2n(seg, scale):
        return seg * (lax.rsqrt(jnp.sum(seg * seg, axis=-1, keepdims=True) + EPS) * scale)

    pv = proj_ext(COL_V, gw)
    q = conv_silu(pq, 0)
    for h in range(GDN_HEADS):
        sl = slice(h * GDN_D, (h + 1) * GDN_D)
        qo_ref[:, sl] = l2n(q[:, sl], GDN_D ** -0.5)
    rest_ref[:, 0:gw] = proj(COL_Z, gw)
    k = conv_silu(pk, 1)
    for h in range(GDN_HEADS):
        sl = slice(h * GDN_D, (h + 1) * GDN_D)
        ko_ref[:, sl] = l2n(k[:, sl], 1.0)
    rest_ref[:, gw:2 * gw] = proj(COL_QC, gw)
    vo_ref[...] = conv_silu(pv, 2)
    tail = proj(COL_VC, P_DIM - COL_VC)
    rest_ref[:, 2 * gw:REST_COLS] = tail[:, :REST_COLS - 2 * gw]

    bd = tail[:, REST_COLS - 2 * gw:]
    lane = lax.broadcasted_iota(jnp.int32, (tm, 128), 1)
    rc = lax.broadcasted_iota(jnp.int32, (tm, 128), 0) & (CHUNK - 1)
    is_g = (lane >= 8) & (lane < 16)
    beta = jax.nn.sigmoid(bd)
    xg = bd + dtb_ref[...]
    softplus = jnp.maximum(xg, 0.0) + jnp.log1p(jnp.exp(-jnp.abs(xg)))
    g = jnp.where(is_g, -jnp.exp(alog_ref[...]) * softplus, 0.0)
    pre = g
    suf = g
    s = 1
    while s < CHUNK:
        pre = pre + jnp.where(rc >= s, pltpu.roll(pre, s, 0), 0.0)
        suf = suf + jnp.where(rc < CHUNK - s, pltpu.roll(suf, tm - s, 0), 0.0)
        s *= 2
    tot = pre + suf - g
    bg = jnp.where(lane < 8, beta,
                   jnp.where(lane < 12, pre,
                             jnp.where(lane < 16, suf, pltpu.roll(tot, 8, 1))))
    bg = jnp.where(lane < 24, bg, 0.0)
    bg_ref[...] = bg
    bgt_ref[0] = jnp.transpose(bg)[0:BG_ROWS, :]


def mix_in(x, mod, layer, g_pre, w_in, conv_w, conv_b, ln_g, ln_b, gdn_conv_w, alog_row, dtb_row, *,
           row0, per_seq, seq_len, tm):
    t = x.shape[0]
    assert seq_len % tm == 0 and per_seq % seq_len == 0 and tm % CONF_ROWS == 0 and tm % CHUNK == 0
    gw = GDN_HEADS * GDN_D
    hb = tm // MIX_HALO
    last = t // MIX_HALO - 1
    tps = seq_len // tm
    kern = functools.partial(_mix_in_kernel, row0=row0, tiles_per_mod=per_seq // tm, tiles_per_seq=tps, tm=tm)
    rowblk = lambda n: pl.BlockSpec((tm, n), lambda i: (i, 0))
    const = lambda shape: _const_spec(shape, lambda i: (0,) * len(shape))
    seq = jax.ShapeDtypeStruct((t, gw), F32)
    return pl.pallas_call(
        kern,
        out_shape=(jax.ShapeDtypeStruct((t, REST_COLS), F32), jax.ShapeDtypeStruct((t, CONV_CH), BF16),
                   seq, seq, seq, jax.ShapeDtypeStruct((t, 128), F32),
                   jax.ShapeDtypeStruct((t // seq_len, BG_ROWS, seq_len), F32)),
        grid=(t // tm,),
        in_specs=[rowblk(D_MODEL),
                  pl.BlockSpec((MIX_HALO, D_MODEL), lambda i: (jnp.maximum(i * hb - 1, 0), 0)),
                  pl.BlockSpec((MIX_HALO, D_MODEL), lambda i: (jnp.minimum((i + 1) * hb, last), 0)),
                  _const_spec((1, 6, mod.shape[2], D_MODEL), lambda i: (layer, 0, 0, 0)),
                  const((1, D_MODEL)), const((D_MODEL, P_DIM)),
                  const((CONV_K, CONV_CH)), const((1, CONV_CH)), const((1, CONV_CH)), const((1, CONV_CH)),
                  const((3, 3 * gw)), const((1, 128)), const((1, 128))],
        out_specs=(rowblk(REST_COLS), rowblk(CONV_CH), rowblk(gw), rowblk(gw), rowblk(gw), rowblk(128),
                   pl.BlockSpec((1, BG_ROWS, tm), lambda i: (i // tps, 0, i % tps))),
        scratch_shapes=[pltpu.VMEM((tm + 2 * MIX_HALO, D_MODEL), BF16),
                        pltpu.VMEM((tm + 2 * MIX_HALO, CONV_CH), F32),
                        pltpu.VMEM((7, tm + 2 * MIX_HALO, CONV_CH), F32)],
        compiler_params=_cparams(("parallel",), VMEM_LIMIT),
        name="mix_in",
    )(x, x, x, mod, g_pre, w_in, conv_w, conv_b, ln_g, ln_b, gdn_conv_w, alog_row, dtb_row)


def _gscan_kernel(qf_ref, kf_ref, vf_ref, bgf_ref, bgtf_ref, qb_ref, kb_ref, vb_ref, bgb_ref, bgtb_ref,
                  s0_ref, of_ref, ob_ref, sfin_ref, s_scr, vn_scr, *, r):
    nb = pl.program_id(1)
    nchunk = r // CHUNK
    shift = int(math.log2(CHUNK))

    @pl.when(nb == 0)
    def _():
        s_scr[...] = s0_ref[0, 0]

    vn_scr[...] = jnp.zeros((2 * GDN_HEADS, r, GDN_D), F32)
    ii = lax.broadcasted_iota(jnp.int32, (r, r), 0)
    jj = lax.broadcasted_iota(jnp.int32, (r, r), 1)
    blk_same = {s: (ii >> s) == (jj >> s) for s in range(3, shift + 1)}
    same = blk_same[shift]
    eye = (ii == jj).astype(F32)
    dirs = ((qf_ref, kf_ref, vf_ref, bgf_ref, bgtf_ref, of_ref), (qb_ref, kb_ref, vb_ref, bgb_ref, bgtb_ref, ob_ref))
    masks = ((same & (ii >= jj), same & (ii > jj)), (same & (ii <= jj), same & (ii < jj)))
    blocks = [(d, h) for d in range(2) for h in range(GDN_HEADS)]
    nblk = len(blocks)

    def each(fn):
        return [fn(i, *blocks[i]) for i in range(nblk)]

    def col(off):
        return each(lambda i, d, h: dirs[d][3][0, :, off + d * GDN_HEADS + h:off + d * GDN_HEADS + h + 1])

    sl_h = [slice(h * GDN_D, (h + 1) * GDN_D) for h in range(GDN_HEADS)]
    beta, gcc, gl = col(0), col(8), col(16)
    gcr = each(lambda i, d, h: dirs[d][4][0, 8 + d * GDN_HEADS + h:9 + d * GDN_HEADS + h, :])
    q = each(lambda i, d, h: dirs[d][0][0, :, sl_h[h]])
    k = each(lambda i, d, h: dirs[d][1][0, :, sl_h[h]])
    v = each(lambda i, d, h: dirs[d][2][0, :, sl_h[h]])
    decay = each(lambda i, d, h: jnp.exp(jnp.where(masks[d][0], gcc[i] - gcr[i], NEG)))
    kbeta = each(lambda i, d, h: k[i] * beta[i])
    kb16 = each(lambda i, d, h: k[i].astype(BF16))
    lm = each(lambda i, d, h: jnp.where(masks[d][1], _dot_nt(kbeta[i].astype(BF16), kb16[i]) * decay[i], 0.0))
    egc = each(lambda i, d, h: jnp.exp(gcc[i]))
    x = each(lambda i, d, h: jnp.concatenate([v[i] * beta[i], kbeta[i] * egc[i]], axis=1))
    l8 = each(lambda i, d, h: jnp.where(blk_same[3], lm[i], 0.0))
    m2 = each(lambda i, d, h: _mm(l8[i], l8[i]))
    m4 = each(lambda i, d, h: _mm(m2[i], m2[i]))
    t = each(lambda i, d, h: eye - l8[i])
    t = each(lambda i, d, h: t[i] + _mm(t[i], m2[i]))
    t = each(lambda i, d, h: t[i] + _mm(t[i], m4[i]))
    for lvl in range(3, shift):
        s_rows = 1 << lvl
        off = blk_same[lvl + 1] & jnp.logical_not(blk_same[lvl])
        ct = each(lambda i, d, h: _mm(_half_rows(jnp.where(off, lm[i], 0.0), s_rows, 1 - d), t[i]))
        upd = each(lambda i, d, h: _mm(_half_rows(t[i], s_rows, 1 - d), _expand_rows(ct[i], s_rows, 1 - d)))
        t = each(lambda i, d, h: t[i] - _expand_rows(upd[i], s_rows, 1 - d))
    x = each(lambda i, d, h: _mm(t[i], x[i]))
    intra = each(lambda i, d, h: (_dot_nt(q[i].astype(BF16), kb16[i]) * decay[i]).astype(BF16))
    qd = each(lambda i, d, h: q[i] * egc[i])
    kd = each(lambda i, d, h: (k[i] * jnp.exp(gl[i] - gcc[i])).astype(BF16))
    eg_last = each(lambda i, d, h: jnp.exp(jnp.broadcast_to(gl[i], (r, GDN_D))))
    state = each(lambda i, d, h: s_scr[d, h])
    outs = [[None] * nchunk for _ in range(nblk)]
    for step in range(nchunk):
        cs = each(lambda i, d, h: step if d == 0 else nchunk - 1 - step)
        sls = [slice(c * CHUNK, (c + 1) * CHUNK) for c in cs]
        a = each(lambda i, d, h: _dot(jnp.concatenate([x[i][sls[i], GDN_D:], qd[i][sls[i]]], axis=0).astype(BF16),
                                      state[i].astype(BF16)))
        v_new = each(lambda i, d, h: x[i][sls[i], :GDN_D] - a[i][:CHUNK])
        for i in range(nblk):
            vn_scr[i, sls[i], :] = v_new[i]
        for i in range(nblk):
            outs[i][cs[i]] = a[i][CHUNK:] + _dot(intra[i][sls[i], :], vn_scr[i].astype(BF16))
        state = each(lambda i, d, h: state[i] * eg_last[i][cs[i] * CHUNK:cs[i] * CHUNK + 1, :]
                     + _dot_tn(kd[i][sls[i]], v_new[i].astype(BF16)))
    for i, (d, h) in enumerate(blocks):
        s_scr[d, h] = state[i]
        dirs[d][5][0, :, sl_h[h]] = jnp.concatenate(outs[i], axis=0)
    sfin_ref[0] = s_scr[...]


def gdn_scan(q, k, v, bg, bgt, state0, layer):
    b, l, gw = q.shape
    r = min(GDN_BLOCK, l)
    nbk = l // r
    kern = functools.partial(_gscan_kernel, r=r)
    fwd = lambda i, n: (i, n, 0)
    bwd = lambda i, n: (i, nbk - 1 - n, 0)
    fwd_t = lambda i, n: (i, 0, n)
    bwd_t = lambda i, n: (i, 0, nbk - 1 - n)
    seq = lambda im: pl.BlockSpec((1, r, gw), im)
    st_shape = (2, GDN_HEADS, GDN_D, GDN_D)
    return pl.pallas_call(
        kern,
        out_shape=(jax.ShapeDtypeStruct((b, l, gw), F32), jax.ShapeDtypeStruct((b, l, gw), F32),
                   jax.ShapeDtypeStruct((b,) + st_shape, F32)),
        grid=(b, nbk),
        in_specs=[seq(fwd), seq(fwd), seq(fwd), pl.BlockSpec((1, r, 128), fwd), pl.BlockSpec((1, BG_ROWS, r), fwd_t),
                  seq(bwd), seq(bwd), seq(bwd), pl.BlockSpec((1, r, 128), bwd), pl.BlockSpec((1, BG_ROWS, r), bwd_t),
                  pl.BlockSpec((1, 1) + st_shape, lambda i, n: (i, layer, 0, 0, 0, 0))],
        out_specs=(seq(fwd), seq(bwd), pl.BlockSpec((1,) + st_shape, lambda i, n: (i, 0, 0, 0, 0))),
        scratch_shapes=[pltpu.VMEM(st_shape, F32), pltpu.VMEM((2 * GDN_HEADS, r, GDN_D), F32)],
        compiler_params=_cparams(("parallel", "arbitrary"), VMEM_LIMIT),
        name="gdn_scan",
    )(q, k, v, bg, bgt, q, k, v, bg, bgt, state0)


def _softmax_pv(q, parts):
    heads = range(NA_HEADS)
    lane = lax.broadcasted_iota(jnp.int32, (1, NA_HEADS * NA_DIM), 1)
    hm = [(lane >= h * NA_DIM) & (lane < (h + 1) * NA_DIM) for h in heads]
    qh = [jnp.where(hm[h], q, 0.0).astype(BF16) for h in heads]
    s = [[_dot_nt(qh[h], keys) if bias is None else _dot_nt(qh[h], keys) + bias(h) for keys, _, bias in parts]
         for h in heads]
    mx = [functools.reduce(jnp.maximum, [jnp.max(sp, axis=-1, keepdims=True) for sp in s[h]]) for h in heads]
    e = [[jnp.exp(sp - mx[h]) for sp in s[h]] for h in heads]
    inv = [1.0 / sum(jnp.sum(ep, axis=-1, keepdims=True) for ep in e[h]) for h in heads]
    o = [sum(_dot(ep.astype(BF16), vals) for ep, (_, vals, _) in zip(e[h], parts)) * inv[h] for h in heads]
    out = o[NA_HEADS - 1]
    for h in range(NA_HEADS - 2, -1, -1):
        out = jnp.where(hm[h], o[h], out)
    return out


def _attn_ctx_kernel(q_ref, k_ref, v_ref, o_ref):
    q = q_ref[0] * (NA_DIM ** -0.5)
    o_ref[0] = _softmax_pv(q, [(k_ref[0].astype(BF16), v_ref[0].astype(BF16), None)]).astype(BF16)


def context_attention(p3):
    b, l, _ = p3.shape
    w = NA_HEADS * NA_DIM
    col = lambda c: pl.BlockSpec((1, l, w), lambda i: (i, 0, c // w))
    return pl.pallas_call(
        _attn_ctx_kernel,
        out_shape=jax.ShapeDtypeStruct((b, l, w), BF16),
        grid=(b,),
        in_specs=[col(REST_QC), col(REST_KC), col(REST_VC)],
        out_specs=pl.BlockSpec((1, l, w), lambda i: (i, 0, 0)),
        compiler_params=_cparams(("parallel",)),
        name="context_attention",
    )(p3, p3, p3)


def _attn_lat_kernel(q_ref, k_ref, v_ref, kc_ref, vc_ref, bias_ref, o_ref, *, rows):
    i = pl.program_id(1)
    nwin = NA_WROWS * GRID_W
    ws = jnp.clip(i * NA_QROWS - WIN_H // 2, 0, rows - NA_WROWS)
    start = pl.multiple_of(ws * GRID_W, GRID_W)
    q = q_ref[0] * (NA_DIM ** -0.5)
    window = (k_ref[0, pl.ds(start, nwin), :].astype(BF16), v_ref[0, pl.ds(start, nwin), :].astype(BF16),
              lambda h: bias_ref[0, h])
    context = (kc_ref[0].astype(BF16), vc_ref[0].astype(BF16), None)
    o_ref[0] = _softmax_pv(q, [window, context]).astype(BF16)


def neighbourhood_attention(p3, k_ctx, v_ctx, bias_tab):
    b, n, _ = p3.shape
    rows = n // GRID_W
    w = NA_HEADS * NA_DIM
    tq = NA_QROWS * GRID_W
    ni = rows // NA_QROWS
    nctx = k_ctx.shape[1]
    kern = functools.partial(_attn_lat_kernel, rows=rows)
    case = lambda i, t: (jnp.where(t == 0, 0, jnp.where(t == ni - 1, 2, 1)), 0, 0, 0)
    return pl.pallas_call(
        kern,
        out_shape=jax.ShapeDtypeStruct((b, n, w), BF16),
        grid=(b, ni),
        in_specs=[pl.BlockSpec((1, tq, w), lambda i, t: (i, t, REST_QC // w)),
                  pl.BlockSpec((1, n, w), lambda i, t: (i, 0, REST_KC // w)),
                  pl.BlockSpec((1, n, w), lambda i, t: (i, 0, REST_VC // w)),
                  pl.BlockSpec((1, nctx, w), lambda i, t: (i, 0, 0)),
                  pl.BlockSpec((1, nctx, w), lambda i, t: (i, 0, 0)),
                  pl.BlockSpec((1, NA_HEADS, tq, NA_WROWS * GRID_W), case)],
        out_specs=pl.BlockSpec((1, tq, w), lambda i, t: (i, t, 0)),
        compiler_params=_cparams(("parallel", "arbitrary"), VMEM_LIMIT),
        name="neighbourhood_attention",
    )(p3, p3, p3, k_ctx, v_ctx, bias_tab)


def _bias_kernel(rpb_ref, o_ref, *, rows):
    l = pl.program_id(0)
    h = pl.program_id(1)
    n_dr, n_dc = 2 * WIN_H - 1, 2 * WIN_W - 1
    base = (l * NA_HEADS + h) * (n_dr * n_dc)
    c = lax.broadcasted_iota(jnp.int32, (GRID_W, 2 * GRID_W), 0)
    lane = lax.broadcasted_iota(jnp.int32, (GRID_W, 2 * GRID_W), 1)
    kc = lane & (GRID_W - 1)
    dc = jnp.clip(kc - c + (WIN_W - 1), 0, n_dc - 1)
    cs = jnp.clip(c - WIN_W // 2, 0, GRID_W - WIN_W)
    col_ok = (kc >= cs) & (kc < cs + WIN_W)
    neg = jnp.full((GRID_W, 2 * GRID_W), NEG, F32)
    tabs = []
    for a in range(n_dr):
        t = jnp.zeros((GRID_W, 2 * GRID_W), F32)
        for bb in range(n_dc):
            t = jnp.where(dc == bb, rpb_ref[base + a * n_dc + bb], t)
        tabs.append(jnp.where(col_ok, t, NEG))
    interior = 2 * NA_QROWS
    cases = ((0, 0), (interior, interior - WIN_H // 2), (rows - NA_QROWS, rows - NA_WROWS))
    for case, (r0, ws) in enumerate(cases):
        for iq in range(NA_QROWS):
            r = r0 + iq
            rs = min(max(r - WIN_H // 2, 0), rows - WIN_H)
            for jp in range(NA_WROWS // 2):
                halves = []
                for half in range(2):
                    kr = ws + 2 * jp + half
                    halves.append(tabs[kr - r + WIN_H - 1] if rs <= kr < rs + WIN_H else neg)
                blk = jnp.where(lane < GRID_W, halves[0], halves[1])
                o_ref[0, case, 0, iq * GRID_W:(iq + 1) * GRID_W, jp * 2 * GRID_W:(jp + 1) * 2 * GRID_W] = blk


def attention_bias_tables(na_rpb, rows):
    depth = na_rpb.shape[0]
    kern = functools.partial(_bias_kernel, rows=rows)
    tq, nw = NA_QROWS * GRID_W, NA_WROWS * GRID_W
    return pl.pallas_call(
        kern,
        out_shape=jax.ShapeDtypeStruct((depth, 3, NA_HEADS, tq, nw), F32),
        grid=(depth, NA_HEADS),
        in_specs=[pl.BlockSpec(memory_space=pltpu.SMEM)],
        out_specs=pl.BlockSpec((1, 3, 1, tq, nw), lambda l, h: (l, 0, h, 0, 0)),
        compiler_params=_cparams(("arbitrary", "arbitrary")),
        name="attention_bias_tables",
    )(na_rpb.reshape(-1))


FFN_HALO = 16
FFN_CHUNK = 256


def _out_ffn_kernel(x_ref, xp_ref, xn_ref, oa_ref, oap_ref, oan_ref, of_ref, ofp_ref, ofn_ref,
                    ob_ref, obp_ref, obn_ref, z_ref, zp_ref, zn_ref, oc_ref, ocp_ref, ocn_ref,
                    wo_ref, wg_ref, wv_ref, wd_ref, cwg_ref, cwv_ref, cbg_ref, cbv_ref,
                    mod_ref, gpm_ref, gpf_ref, gnw_ref, gpo_ref, o_ref,
                    mix_scr, hext_scr, act_scr, *, row0, tiles_per_mod, tiles_per_seq, tm):
    i = pl.program_id(0)
    row = row0 + i // tiles_per_mod
    g1 = mod_ref[0, 2, pl.ds(row, 1), :]
    sh2 = mod_ref[0, 3, pl.ds(row, 1), :]
    sc2 = mod_ref[0, 4, pl.ds(row, 1), :]
    g2 = mod_ref[0, 5, pl.ds(row, 1), :]
    hh = FFN_HALO
    gw = GDN_HEADS * GDN_D
    parts = ((0, tm, x_ref, oa_ref, of_ref, ob_ref, z_ref, oc_ref),
             (tm, hh, xp_ref, oap_ref, ofp_ref, obp_ref, zp_ref, ocp_ref),
             (tm + hh, hh, xn_ref, oan_ref, ofn_ref, obn_ref, zn_ref, ocn_ref))
    for r0, n, _, a_ref, f_ref, b_ref, zz_ref, c_ref in parts:
        mix_scr[r0:r0 + n, 0:CONV_CH] = a_ref[...]
        og = f_ref[...] + b_ref[...]
        zz = zz_ref[...]
        for h in range(GDN_HEADS):
            sl = slice(h * GDN_D, (h + 1) * GDN_D)
            gated = _rms(og[:, sl], gnw_ref[...]) * _silu(zz[:, sl])
            mix_scr[r0:r0 + n, CONV_CH + h * GDN_D:CONV_CH + (h + 1) * GDN_D] = gated.astype(BF16)
        mix_scr[r0:r0 + n, CONV_CH + gw:] = c_ref[...]
    m = _dot(mix_scr[...], wo_ref[...])
    x1 = None
    for r0, n, xx_ref, *_ in parts:
        x1_part = xx_ref[...] + g1 * _rms(m[r0:r0 + n], gpm_ref[...])
        hext_scr[r0:r0 + n, :] = (_rms(x1_part, gpf_ref[...]) * (1.0 + sc2) + sh2).astype(BF16)
        if r0 == 0:
            x1 = x1_part
    t_in_seq = i % tiles_per_seq
    keep_prev = (t_in_seq > 0).astype(F32)
    keep_next = (t_in_seq < tiles_per_seq - 1).astype(F32)
    row8 = lax.broadcasted_iota(jnp.int32, (8, FFN_CHUNK), 0)

    def conv(u_ext, cw_ref, cb_ref, sl):
        u = u_ext[0:tm]
        prev_row = u_ext[tm + hh - 1:tm + hh] * keep_prev
        next_row = u_ext[tm + hh:tm + hh + 1] * keep_next
        up = pltpu.roll(u, 1, 0)
        dn = pltpu.roll(u, tm - 1, 0)
        up = jnp.concatenate([jnp.where(row8 == 0, prev_row, up[0:8]), up[8:]], axis=0)
        dn = jnp.concatenate([dn[:tm - 8], jnp.where(row8 == 7, next_row, dn[tm - 8:])], axis=0)
        return up * cw_ref[0:1, sl] + u * cw_ref[1:2, sl] + dn * cw_ref[2:3, sl] + cb_ref[:, sl]

    n_chunks = D_FF // FFN_CHUNK
    sls = [slice(c * FFN_CHUNK, (c + 1) * FFN_CHUNK) for c in range(n_chunks)]

    def up_proj(c):
        return _dot(hext_scr[...], wg_ref[:, sls[c]]), _dot(hext_scr[...], wv_ref[:, sls[c]])

    nxt = up_proj(0)
    for c in range(n_chunks):
        ug, uv = nxt
        if c + 1 < n_chunks:
            nxt = up_proj(c + 1)
        gate = conv(ug, cwg_ref, cbg_ref, sls[c])
        val = conv(uv, cwv_ref, cbv_ref, sls[c])
        act_scr[:, sls[c]] = (_silu(gate) * val).astype(BF16)
    f = _dot(act_scr[...], wd_ref[...])
    o_ref[...] = x1 + g2 * _rms(f, gpo_ref[...])


def out_ffn(x, oa, of, ob, rest, oc, w_out, wg, wv, wd, cwg, cwv, cbg, cbv, mod, layer, g_post_mix, g_pre_ffn, gnw,
            g_post_ffn, *, row0, per_seq, seq_len, tm):
    t = x.shape[0]
    assert seq_len % tm == 0 and per_seq % seq_len == 0
    gw = GDN_HEADS * GDN_D
    hb = tm // FFN_HALO
    last = t // FFN_HALO - 1
    kern = functools.partial(_out_ffn_kernel, row0=row0, tiles_per_mod=per_seq // tm, tiles_per_seq=seq_len // tm, tm=tm)

    def trio(n):
        return [pl.BlockSpec((tm, n), lambda i: (i, 0)),
                pl.BlockSpec((FFN_HALO, n), lambda i: (jnp.maximum(i * hb - 1, 0), 0)),
                pl.BlockSpec((FFN_HALO, n), lambda i: (jnp.minimum((i + 1) * hb, last), 0))]

    const = lambda shape: _const_spec(shape, lambda i: (0,) * len(shape))
    return pl.pallas_call(
        kern,
        out_shape=jax.ShapeDtypeStruct((t, D_MODEL), F32),
        grid=(t // tm,),
        in_specs=(trio(D_MODEL) + trio(CONV_CH) + trio(gw) + trio(gw) + trio(gw) + trio(NA_HEADS * NA_DIM)
                  + [const((D_MODEL, D_MODEL)), const((D_MODEL, D_FF)), const((D_MODEL, D_FF)), const((D_FF, D_MODEL)),
                     const((3, D_FF)), const((3, D_FF)), const((1, D_FF)), const((1, D_FF)),
                     _const_spec((1, 6, mod.shape[2], D_MODEL), lambda i: (layer, 0, 0, 0)),
                     const((1, D_MODEL)), const((1, D_MODEL)), const((1, GDN_D)), const((1, D_MODEL))]),
        out_specs=pl.BlockSpec((tm, D_MODEL), lambda i: (i, 0)),
        scratch_shapes=[pltpu.VMEM((tm + 2 * FFN_HALO, D_MODEL), BF16), pltpu.VMEM((tm + 2 * FFN_HALO, D_MODEL), BF16),
                        pltpu.VMEM((tm, D_FF), BF16)],
        compiler_params=_cparams(("parallel",), VMEM_LIMIT),
        name="out_ffn",
    )(x, x, x, oa, oa, oa, of, of, of, ob, ob, ob, rest, rest, rest, oc, oc, oc, w_out, wg, wv, wd, cwg, cwv, cbg, cbv,
      mod, g_post_mix, g_pre_ffn, gnw, g_post_ffn)


def _trunk_layer(x, lw, layer, mod, *, batch, seq, row0, per_seq, ctx):
    tm = min(512, seq)
    rest, oa, qn, kn, vn, bg, bgt = mix_in(
        x, mod, layer, lw["g_pre_mix"], lw["w_in"], lw["conv_w"], lw["conv_b"], lw["conv_ln_g"], lw["conv_ln_b"],
        lw["gdn_conv_w"], lw["alog_row"], lw["dtb_row"], row0=row0, per_seq=per_seq, seq_len=seq, tm=tm)
    gw = GDN_HEADS * GDN_D
    p3 = rest.reshape(batch, seq, REST_COLS)
    qn, kn, vn = (a.reshape(batch, seq, gw) for a in (qn, kn, vn))
    bg = bg.reshape(batch, seq, 128)
    if ctx is None:
        state0 = jnp.zeros((batch, 1, 2, GDN_HEADS, GDN_D, GDN_D), F32)
        of, ob, s_fin = gdn_scan(qn, kn, vn, bg, bgt, state0, 0)
        oc = context_attention(p3)
    else:
        k_ctx, v_ctx, bias_tab, state_delta = ctx
        of, ob, s_fin = gdn_scan(qn, kn, vn, bg, bgt, state_delta, layer)
        oc = neighbourhood_attention(p3, k_ctx, v_ctx, bias_tab)
    t = batch * seq
    x2 = out_ffn(x, oa, of.reshape(t, gw), ob.reshape(t, gw), rest, oc.reshape(t, NA_HEADS * NA_DIM), lw["w_out"],
                 lw["wg"], lw["wv"], lw["wd"], lw["cwg"], lw["cwv"], lw["cbg"], lw["cbv"], mod, layer,
                 lw["g_post_mix"], lw["g_pre_ffn"], lw["gdn_norm_w"], lw["g_post_ffn"],
                 row0=row0, per_seq=per_seq, seq_len=seq, tm=tm)
    return x2, p3, s_fin


def kernel(x_prompt, x_sample, cache_attn_kv, state_delta, c, c_ctx, w_ada, b_ada, g_pre_mix, g_post_mix, g_pre_ffn, g_post_ffn, w_in, w_out, conv_w, conv_b, conv_ln_g, conv_ln_b, gdn_conv_w, gdn_a_log, gdn_dt_bias, gdn_norm_w, na_rpb, w_up, ffn_conv_w, ffn_conv_b, w_down):
    depth = w_in.shape[0]
    bp, sp, d = x_prompt.shape
    bs, ss, _ = x_sample.shape
    rows = ss // GRID_W
    nmod = 16
    assert 1 + bs <= nmod

    n_gdn = 2 * GDN_HEADS
    c0 = 2 * CONV_CH + 4 * GDN_HEADS * GDN_D
    w_in_r = jnp.concatenate(
        [w_in[:, :, :c0], w_in[:, :, c0 + 2 * n_gdn:], w_in[:, :, c0:c0 + 2 * n_gdn],
         jnp.zeros((depth, d, P_DIM - w_in.shape[2]), w_in.dtype)], axis=-1).astype(BF16)
    w_out_b = w_out.astype(BF16)
    wg_b = w_up[:, :, :D_FF].astype(BF16)
    wv_b = w_up[:, :, D_FF:].astype(BF16)
    wd_b = w_down.astype(BF16)

    def lane_row(vals):
        return jnp.pad(vals.reshape(depth, 1, n_gdn), ((0, 0), (0, 0), (8, 128 - 8 - n_gdn)))

    alog_rows = lane_row(gdn_a_log)
    dtb_rows = lane_row(gdn_dt_bias)

    cvecs = jnp.concatenate([c_ctx[None, :], c, jnp.zeros((nmod - 1 - bs, d), F32)], axis=0)
    mod = ada_modulation(cvecs, w_ada, b_ada)
    bias_tabs = attention_bias_tables(na_rpb, rows)

    xp = x_prompt.reshape(bp * sp, d)
    xs = x_sample.reshape(bs * ss, d)
    kv_new, st_new = [], []
    hw = NA_HEADS * NA_DIM
    for l in range(depth):
        lw = {
            "g_pre_mix": g_pre_mix[l][None], "g_post_mix": g_post_mix[l][None],
            "g_pre_ffn": g_pre_ffn[l][None], "g_post_ffn": g_post_ffn[l][None],
            "w_in": w_in_r[l], "w_out": w_out_b[l],
            "conv_w": conv_w[l], "conv_b": conv_b[l][None], "conv_ln_g": conv_ln_g[l][None],
            "conv_ln_b": conv_ln_b[l][None],
            "gdn_conv_w": gdn_conv_w[l], "alog_row": alog_rows[l], "dtb_row": dtb_rows[l],
            "gdn_norm_w": gdn_norm_w[l][None],
            "wg": wg_b[l], "wv": wv_b[l], "wd": wd_b[l],
            "cwg": ffn_conv_w[l][:, :D_FF], "cwv": ffn_conv_w[l][:, D_FF:],
            "cbg": ffn_conv_b[l][None, :D_FF], "cbv": ffn_conv_b[l][None, D_FF:],
        }
        xp, p3, s_fin = _trunk_layer(xp, lw, l, mod, batch=bp, seq=sp, row0=0, per_seq=bp * sp, ctx=None)
        heads = lambda col: p3[:, :, col:col + hw].reshape(bp, sp, NA_HEADS, NA_DIM).transpose(0, 2, 1, 3)
        kv_new.append(jnp.stack([heads(REST_KC), heads(REST_VC)], axis=1))
        st_new.append(s_fin)
        lanes = lambda t: t.transpose(0, 2, 1, 3).reshape(bs, t.shape[2], hw)
        ctx = (lanes(cache_attn_kv[:, l, 0]), lanes(cache_attn_kv[:, l, 1]), bias_tabs[l], state_delta)
        xs, _, _ = _trunk_layer(xs, lw, l, mod, batch=bs, seq=ss, row0=1, per_seq=ss, ctx=ctx)
    return (xp.reshape(bp, sp, d), xs.reshape(bs, ss, d), jnp.stack(kv_new, axis=1), jnp.stack(st_new, axis=1))
```

```python
import functools
import math

import jax
import jax.numpy as jnp
from jax import lax
from jax.experimental import pallas as pl
from jax.experimental.pallas import tpu as pltpu

F32 = jnp.float32
BF16 = jnp.bfloat16
EPS = 1e-6
NEG = -1e30

D_MODEL = 1024
CONV_CH = 256
CONV_K = 31
GDN_HEADS = 4
GDN_D = 128
NA_HEADS = 4
NA_DIM = 64
GRID_W = 64
WIN_H = 8
WIN_W = 16
D_FF = 2816
CHUNK = 64
GDN_BLOCK = 512
GDN_SUB = 256
NA_QROWS = 4
NA_WROWS = 12

COL_AV, COL_AG = 0, 256
COL_Q, COL_K, COL_V, COL_Z = 512, 1024, 1536, 2048
COL_QC, COL_KC, COL_VC = 2560, 2816, 3072
COL_BD = 3328
P_DIM = 3456

VMEM_LIMIT = 56 * 1024 * 1024


def _cparams(sem, vmem=None):
    return pltpu.CompilerParams(dimension_semantics=sem, vmem_limit_bytes=vmem)


def _const_spec(shape, index_map):
    return pl.BlockSpec(shape, index_map, pipeline_mode=pl.Buffered(1))


def _silu(x):
    return x * jax.nn.sigmoid(x)


def _rms(x, g):
    return x * lax.rsqrt(jnp.mean(x * x, axis=-1, keepdims=True) + EPS) * g


def _dot(a, b):
    return jnp.dot(a, b, preferred_element_type=F32)


def _dot_nt(a, b):
    return lax.dot_general(a, b, (((1,), (1,)), ((), ())), preferred_element_type=F32)


def _dot_tn(a, b):
    return lax.dot_general(a, b, (((0,), (0,)), ((), ())), preferred_element_type=F32)


def _mm(a, b):
    return _dot(a.astype(BF16), b.astype(BF16))


def _half_rows(x, s, half):
    n, c = x.shape[0] // (2 * s), x.shape[1]
    return x.reshape(n, 2, s, c)[:, half].reshape(n * s, c)


def _expand_rows(y, s, half):
    n, c = y.shape[0] // s, y.shape[1]
    y4 = y.reshape(n, 1, s, c)
    z = jnp.zeros_like(y4)
    return jnp.concatenate([z, y4] if half == 1 else [y4, z], axis=1).reshape(2 * n * s, c)


def _ada_kernel(c_ref, w_ref, b_ref, o_ref):
    s = _silu(c_ref[...]).astype(BF16)
    o_ref[0, 0] = _dot(s, w_ref[0].astype(BF16)) + b_ref[0, 0]


def ada_modulation(cvecs, w_ada, b_ada):
    depth = w_ada.shape[0]
    n = cvecs.shape[0]
    b4 = b_ada.reshape(depth, 6, 1, D_MODEL)
    return pl.pallas_call(
        _ada_kernel,
        out_shape=jax.ShapeDtypeStruct((depth, 6, n, D_MODEL), F32),
        grid=(depth, 6),
        in_specs=[pl.BlockSpec((n, D_MODEL), lambda l, j: (0, 0)),
                  pl.BlockSpec((1, D_MODEL, D_MODEL), lambda l, j: (l, 0, j)),
                  pl.BlockSpec((1, 1, 1, D_MODEL), lambda l, j: (l, j, 0, 0))],
        out_specs=pl.BlockSpec((1, 1, n, D_MODEL), lambda l, j: (l, j, 0, 0)),
        compiler_params=_cparams(("arbitrary", "arbitrary")),
        name="ada_modulation",
    )(cvecs, w_ada, b4)


MIX_HALO = 16
CONF_ROWS = 64
BG_ROWS = 24
REST_COLS = COL_BD - COL_Z
REST_Z, REST_QC, REST_KC, REST_VC = 0, COL_QC - COL_Z, COL_KC - COL_Z, COL_VC - COL_Z


def _mix_in_kernel(x_ref, xp_ref, xn_ref, mod_ref, g_ref, w_ref, cw_ref, cb_ref, lg_ref, lb_ref, gcw_ref,
                   alog_ref, dtb_ref, rest_ref, oa_ref, qo_ref, ko_ref, vo_ref, bg_ref, bgt_ref,
                   hext_scr, u_scr, rot_scr, *, row0, tiles_per_mod, tiles_per_seq, tm):
    i = pl.program_id(0)
    row = row0 + i // tiles_per_mod
    t_in_seq = i % tiles_per_seq
    keep_prev = (t_in_seq > 0).astype(F32)
    keep_next = (t_in_seq < tiles_per_seq - 1).astype(F32)
    sh = mod_ref[0, 0, pl.ds(row, 1), :]
    sc = mod_ref[0, 1, pl.ds(row, 1), :]
    hh = MIX_HALO
    gw = GDN_HEADS * GDN_D

    def norm(x):
        return (_rms(x, g_ref[...]) * (1.0 + sc) + sh).astype(BF16)

    hext_scr[0:hh, :] = norm(xp_ref[...])
    hext_scr[hh:hh + tm, :] = norm(x_ref[...])
    hext_scr[hh + tm:2 * hh + tm, :] = norm(xn_ref[...])

    def proj_ext(col, width):
        return _dot(hext_scr[...], w_ref[:, col:col + width])

    def proj(col, width):
        return _dot(hext_scr[hh:hh + tm, :], w_ref[:, col:col + width])

    pa = proj_ext(COL_AV, 2 * CONV_CH)
    pq = proj_ext(COL_Q, gw)
    u = pa[:, :CONV_CH] * jax.nn.sigmoid(pa[:, CONV_CH:])
    u_scr[0:hh, :] = u[0:hh] * keep_prev
    u_scr[hh:hh + tm, :] = u[hh:hh + tm]
    u_scr[hh + tm:2 * hh + tm, :] = u[hh + tm:] * keep_next
    n_ext = tm + 2 * hh
    u_ext = u_scr[...]
    for r in range(1, 8):
        rot_scr[r - 1] = pltpu.roll(u_ext, n_ext - r, 0)
    pk = proj_ext(COL_K, gw)
    half = CONV_K // 2
    for r0 in range(0, tm, CONF_ROWS):
        acc = jnp.zeros((CONF_ROWS, CONV_CH), F32) + cb_ref[...]
        for k in range(CONV_K):
            off = r0 + k + hh - half
            r = off % 8
            src = u_scr[off:off + CONF_ROWS, :] if r == 0 else rot_scr[r - 1, off - r:off - r + CONF_ROWS, :]
            acc = acc + cw_ref[pl.ds(k, 1), :] * src
        mu = jnp.mean(acc, axis=-1, keepdims=True)
        cen = acc - mu
        var = jnp.mean(cen * cen, axis=-1, keepdims=True)
        y = cen * lax.rsqrt(var + EPS) * lg_ref[...] + lb_ref[...]
        oa_ref[r0:r0 + CONF_ROWS, :] = _silu(y).astype(BF16)

    row8 = lax.broadcasted_iota(jnp.int32, (8, gw), 0)

    def conv_silu(pe, j):
        x = pe[hh:hh + tm]
        prev_row = pe[hh - 1:hh] * keep_prev
        next_row = pe[hh + tm:hh + tm + 1] * keep_next
        up = pltpu.roll(x, 1, 0)
        dn = pltpu.roll(x, tm - 1, 0)
        up = jnp.concatenate([jnp.where(row8 == 0, prev_row, up[0:8]), up[8:]], axis=0)
        dn = jnp.concatenate([dn[:tm - 8], jnp.where(row8 == 7, next_row, dn[tm - 8:])], axis=0)
        w = gcw_ref[:, j * gw:(j + 1) * gw]
        return _silu(up * w[0:1, :] + x * w[1:2, :] + dn * w[2:3, :])

    def l2n(seg, scale):
        return seg * (lax.rsqrt(jnp.sum(seg * seg, axis=-1, keepdims=True) + EPS) * scale)

    pv = proj_ext(COL_V, gw)
    q = conv_silu(pq, 0)
    for h in range(GDN_HEADS):
        sl = slice(h * GDN_D, (h + 1) * GDN_D)
        qo_ref[:, sl] = l2n(q[:, sl], GDN_D ** -0.5)
    rest_ref[:, 0:gw] = proj(COL_Z, gw)
    k = conv_silu(pk, 1)
    for h in range(GDN_HEADS):
        sl = slice(h * GDN_D, (h + 1) * GDN_D)
        ko_ref[:, sl] = l2n(k[:, sl], 1.0)
    rest_ref[:, gw:2 * gw] = proj(COL_QC, gw)
    vo_ref[...] = conv_silu(pv, 2)
    tail = proj(COL_VC, P_DIM - COL_VC)
    rest_ref[:, 2 * gw:REST_COLS] = tail[:, :REST_COLS - 2 * gw]

    bd = tail[:, REST_COLS - 2 * gw:]
    lane = lax.broadcasted_iota(jnp.int32, (tm, 128), 1)
    rc = lax.broadcasted_iota(jnp.int32, (tm, 128), 0) & (CHUNK - 1)
    is_g = (lane >= 8) & (lane < 16)
    beta = jax.nn.sigmoid(bd)
    xg = bd + dtb_ref[...]
    softplus = jnp.maximum(xg, 0.0) + jnp.log1p(jnp.exp(-jnp.abs(xg)))
    g = jnp.where(is_g, -jnp.exp(alog_ref[...]) * softplus, 0.0)
    pre = g
    suf = g
    s = 1
    while s < CHUNK:
        pre = pre + jnp.where(rc >= s, pltpu.roll(pre, s, 0), 0.0)
        suf = suf + jnp.where(rc < CHUNK - s, pltpu.roll(suf, tm - s, 0), 0.0)
        s *= 2
    tot = pre + suf - g
    bg = jnp.where(lane < 8, beta,
                   jnp.where(lane < 12, pre,
                             jnp.where(lane < 16, suf, pltpu.roll(tot, 8, 1))))
    bg = jnp.where(lane < 24, bg, 0.0)
    bg_ref[...] = bg
    bgt_ref[0] = jnp.transpose(bg)[0:BG_ROWS, :]


def mix_in(x, mod, layer, g_pre, w_in, conv_w, conv_b, ln_g, ln_b, gdn_conv_w, alog_row, dtb_row, *,
           row0, per_seq, seq_len, tm):
    t = x.shape[0]
    assert seq_len % tm == 0 and per_seq % seq_len == 0 and tm % CONF_ROWS == 0 and tm % CHUNK == 0
    gw = GDN_HEADS * GDN_D
    hb = tm // MIX_HALO
    last = t // MIX_HALO - 1
    tps = seq_len // tm
    kern = functools.partial(_mix_in_kernel, row0=row0, tiles_per_mod=per_seq // tm, tiles_per_seq=tps, tm=tm)
    rowblk = lambda n: pl.BlockSpec((tm, n), lambda i: (i, 0))
    const = lambda shape: _const_spec(shape, lambda i: (0,) * len(shape))
    seq = jax.ShapeDtypeStruct((t, gw), F32)
    return pl.pallas_call(
        kern,
        out_shape=(jax.ShapeDtypeStruct((t, REST_COLS), F32), jax.ShapeDtypeStruct((t, CONV_CH), BF16),
                   seq, seq, seq, jax.ShapeDtypeStruct((t, 128), F32),
                   jax.ShapeDtypeStruct((t // seq_len, BG_ROWS, seq_len), F32)),
        grid=(t // tm,),
        in_specs=[rowblk(D_MODEL),
                  pl.BlockSpec((MIX_HALO, D_MODEL), lambda i: (jnp.maximum(i * hb - 1, 0), 0)),
                  pl.BlockSpec((MIX_HALO, D_MODEL), lambda i: (jnp.minimum((i + 1) * hb, last), 0)),
                  _const_spec((1, 6, mod.shape[2], D_MODEL), lambda i: (layer, 0, 0, 0)),
                  const((1, D_MODEL)), const((D_MODEL, P_DIM)),
                  const((CONV_K, CONV_CH)), const((1, CONV_CH)), const((1, CONV_CH)), const((1, CONV_CH)),
                  const((3, 3 * gw)), const((1, 128)), const((1, 128))],
        out_specs=(rowblk(REST_COLS), rowblk(CONV_CH), rowblk(gw), rowblk(gw), rowblk(gw), rowblk(128),
                   pl.BlockSpec((1, BG_ROWS, tm), lambda i: (i // tps, 0, i % tps))),
        scratch_shapes=[pltpu.VMEM((tm + 2 * MIX_HALO, D_MODEL), BF16),
                        pltpu.VMEM((tm + 2 * MIX_HALO, CONV_CH), F32),
                        pltpu.VMEM((7, tm + 2 * MIX_HALO, CONV_CH), F32)],
        compiler_params=_cparams(("parallel",), VMEM_LIMIT),
        name="mix_in",
    )(x, x, x, mod, g_pre, w_in, conv_w, conv_b, ln_g, ln_b, gdn_conv_w, alog_row, dtb_row)


def _gscan_kernel(qf_ref, kf_ref, vf_ref, bgf_ref, bgtf_ref, qb_ref, kb_ref, vb_ref, bgb_ref, bgtb_ref,
                  s0_ref, of_ref, ob_ref, sfin_ref, s_scr, vn_scr, *, r):
    nb = pl.program_id(1)
    sub = min(GDN_SUB, r)
    nsub = r // sub
    nchunk = sub // CHUNK
    shift = int(math.log2(CHUNK))

    @pl.when(nb == 0)
    def _():
        s_scr[...] = s0_ref[0, 0]

    vn_scr[...] = jnp.zeros((2 * GDN_HEADS, sub, GDN_D), F32)
    ii = lax.broadcasted_iota(jnp.int32, (sub, sub), 0)
    jj = lax.broadcasted_iota(jnp.int32, (sub, sub), 1)
    blk_same = {s: (ii >> s) == (jj >> s) for s in range(3, shift + 1)}
    same = blk_same[shift]
    eye = (ii == jj).astype(F32)
    dirs = ((qf_ref, kf_ref, vf_ref, bgf_ref, bgtf_ref, of_ref), (qb_ref, kb_ref, vb_ref, bgb_ref, bgtb_ref, ob_ref))
    masks = ((same & (ii >= jj), same & (ii > jj)), (same & (ii <= jj), same & (ii < jj)))
    sl_h = [slice(h * GDN_D, (h + 1) * GDN_D) for h in range(GDN_HEADS)]
    nblk = 2 * GDN_HEADS

    def prepare(group, out):
        rs = [slice(sb * sub, (sb + 1) * sub) for sb, d, h in group]

        def each(fn):
            return [fn(i, group[i][1], group[i][2]) for i in range(nblk)]

        def col(off):
            return each(lambda i, d, h: dirs[d][3][0, rs[i], off + d * GDN_HEADS + h:off + d * GDN_HEADS + h + 1])

        beta, gcc, gl = col(0), col(8), col(16)
        gcr = each(lambda i, d, h: dirs[d][4][0, 8 + d * GDN_HEADS + h:9 + d * GDN_HEADS + h, rs[i]])
        q = each(lambda i, d, h: dirs[d][0][0, rs[i], sl_h[h]])
        k = each(lambda i, d, h: dirs[d][1][0, rs[i], sl_h[h]])
        v = each(lambda i, d, h: dirs[d][2][0, rs[i], sl_h[h]])
        decay = each(lambda i, d, h: jnp.exp(jnp.where(masks[d][0], gcc[i] - gcr[i], NEG)))
        kbeta = each(lambda i, d, h: k[i] * beta[i])
        kb16 = each(lambda i, d, h: k[i].astype(BF16))
        yield
        lm = each(lambda i, d, h: jnp.where(masks[d][1], _dot_nt(kbeta[i].astype(BF16), kb16[i]) * decay[i], 0.0))
        egc = each(lambda i, d, h: jnp.exp(gcc[i]))
        x = each(lambda i, d, h: jnp.concatenate([v[i] * beta[i], kbeta[i] * egc[i]], axis=1))
        yield
        l8 = each(lambda i, d, h: jnp.where(blk_same[3], lm[i], 0.0))
        m2 = each(lambda i, d, h: _mm(l8[i], l8[i]))
        yield
        m4 = each(lambda i, d, h: _mm(m2[i], m2[i]))
        t = each(lambda i, d, h: eye - l8[i])
        t = each(lambda i, d, h: t[i] + _mm(t[i], m2[i]))
        yield
        t = each(lambda i, d, h: t[i] + _mm(t[i], m4[i]))
        yield
        for lvl in range(3, shift):
            s_rows = 1 << lvl
            off = blk_same[lvl + 1] & jnp.logical_not(blk_same[lvl])
            ct = each(lambda i, d, h: _mm(_half_rows(jnp.where(off, lm[i], 0.0), s_rows, 1 - d), t[i]))
            yield
            upd = each(lambda i, d, h: _mm(_half_rows(t[i], s_rows, 1 - d), _expand_rows(ct[i], s_rows, 1 - d)))
            t = each(lambda i, d, h: t[i] - _expand_rows(upd[i], s_rows, 1 - d))
            yield
        out["x"] = each(lambda i, d, h: _mm(t[i], x[i]))
        yield
        out["intra"] = each(lambda i, d, h: (_dot_nt(q[i].astype(BF16), kb16[i]) * decay[i]).astype(BF16))
        out["qd"] = each(lambda i, d, h: q[i] * egc[i])
        out["kd"] = each(lambda i, d, h: (k[i] * jnp.exp(gl[i] - gcc[i])).astype(BF16))
        out["eg_last"] = each(lambda i, d, h: jnp.exp(jnp.broadcast_to(gl[i], (sub, GDN_D))))
        yield

    def scan(group, pre, state):
        x, qd, kd, intra, eg_last = pre["x"], pre["qd"], pre["kd"], pre["intra"], pre["eg_last"]
        outs = [[None] * nchunk for _ in range(nblk)]
        for step in range(nchunk):
            cs = [step if d == 0 else nchunk - 1 - step for sb, d, h in group]
            sls = [slice(c * CHUNK, (c + 1) * CHUNK) for c in cs]
            a = [_dot(jnp.concatenate([x[i][sls[i], GDN_D:], qd[i][sls[i]]], axis=0).astype(BF16),
                      state[group[i][1:]].astype(BF16)) for i in range(nblk)]
            v_new = [x[i][sls[i], :GDN_D] - a[i][:CHUNK] for i in range(nblk)]
            for i in range(nblk):
                vn_scr[i, sls[i], :] = v_new[i]
            yield
            for i in range(nblk):
                outs[i][cs[i]] = a[i][CHUNK:] + _dot(intra[i][sls[i], :], vn_scr[i].astype(BF16))
            for i in range(nblk):
                key = group[i][1:]
                state[key] = (state[key] * eg_last[i][cs[i] * CHUNK:cs[i] * CHUNK + 1, :]
                              + _dot_tn(kd[i][sls[i]], v_new[i].astype(BF16)))
            yield
        for i, (sb, d, h) in enumerate(group):
            dirs[d][5][0, sb * sub:(sb + 1) * sub, sl_h[h]] = jnp.concatenate(outs[i], axis=0)

    def run(*gens):
        gens = [g for g in gens if g is not None]
        while gens:
            for g in list(gens):
                try:
                    next(g)
                except StopIteration:
                    gens.remove(g)

    groups = [[(p, 0, h) for h in range(GDN_HEADS)] + [(nsub - 1 - p, 1, h) for h in range(GDN_HEADS)]
              for p in range(nsub)]
    state = {(d, h): s_scr[d, h] for d in range(2) for h in range(GDN_HEADS)}
    prev = None
    for p in range(nsub):
        pre = {}
        run(prepare(groups[p], pre), scan(groups[p - 1], prev, state) if p else None)
        prev = pre
    run(scan(groups[nsub - 1], prev, state))
    for (d, h), val in state.items():
        s_scr[d, h] = val
    sfin_ref[0] = s_scr[...]


def gdn_scan(q, k, v, bg, bgt, state0, layer):
    b, l, gw = q.shape
    r = min(GDN_BLOCK, l)
    nbk = l // r
    kern = functools.partial(_gscan_kernel, r=r)
    fwd = lambda i, n: (i, n, 0)
    bwd = lambda i, n: (i, nbk - 1 - n, 0)
    fwd_t = lambda i, n: (i, 0, n)
    bwd_t = lambda i, n: (i, 0, nbk - 1 - n)
    seq = lambda im: pl.BlockSpec((1, r, gw), im)
    st_shape = (2, GDN_HEADS, GDN_D, GDN_D)
    return pl.pallas_call(
        kern,
        out_shape=(jax.ShapeDtypeStruct((b, l, gw), F32), jax.ShapeDtypeStruct((b, l, gw), F32),
                   jax.ShapeDtypeStruct((b,) + st_shape, F32)),
        grid=(b, nbk),
        in_specs=[seq(fwd), seq(fwd), seq(fwd), pl.BlockSpec((1, r, 128), fwd), pl.BlockSpec((1, BG_ROWS, r), fwd_t),
                  seq(bwd), seq(bwd), seq(bwd), pl.BlockSpec((1, r, 128), bwd), pl.BlockSpec((1, BG_ROWS, r), bwd_t),
                  pl.BlockSpec((1, 1) + st_shape, lambda i, n: (i, layer, 0, 0, 0, 0))],
        out_specs=(seq(fwd), seq(bwd), pl.BlockSpec((1,) + st_shape, lambda i, n: (i, 0, 0, 0, 0))),
        scratch_shapes=[pltpu.VMEM(st_shape, F32), pltpu.VMEM((2 * GDN_HEADS, min(GDN_SUB, r), GDN_D), F32)],
        compiler_params=_cparams(("parallel", "arbitrary"), VMEM_LIMIT),
        name="gdn_scan",
    )(q, k, v, bg, bgt, q, k, v, bg, bgt, state0)


def _softmax_pv(q, parts):
    heads = range(NA_HEADS)
    lane = lax.broadcasted_iota(jnp.int32, (1, NA_HEADS * NA_DIM), 1)
    hm = [(lane >= h * NA_DIM) & (lane < (h + 1) * NA_DIM) for h in heads]
    qh = [jnp.where(hm[h], q, 0.0).astype(BF16) for h in heads]
    s = [[_dot_nt(qh[h], keys) if bias is None else _dot_nt(qh[h], keys) + bias(h) for keys, _, bias in parts]
         for h in heads]
    mx = [functools.reduce(jnp.maximum, [jnp.max(sp, axis=-1, keepdims=True) for sp in s[h]]) for h in heads]
    e = [[jnp.exp(sp - mx[h]) for sp in s[h]] for h in heads]
    inv = [1.0 / sum(jnp.sum(ep, axis=-1, keepdims=True) for ep in e[h]) for h in heads]
    o = [sum(_dot(ep.astype(BF16), vals) for ep, (_, vals, _) in zip(e[h], parts)) * inv[h] for h in heads]
    out = o[NA_HEADS - 1]
    for h in range(NA_HEADS - 2, -1, -1):
        out = jnp.where(hm[h], o[h], out)
    return out


def _attn_ctx_kernel(q_ref, k_ref, v_ref, o_ref):
    q = q_ref[0] * (NA_DIM ** -0.5)
    o_ref[0] = _softmax_pv(q, [(k_ref[0].astype(BF16), v_ref[0].astype(BF16), None)]).astype(BF16)


def context_attention(p3):
    b, l, _ = p3.shape
    w = NA_HEADS * NA_DIM
    col = lambda c: pl.BlockSpec((1, l, w), lambda i: (i, 0, c // w))
    return pl.pallas_call(
        _attn_ctx_kernel,
        out_shape=jax.ShapeDtypeStruct((b, l, w), BF16),
        grid=(b,),
        in_specs=[col(REST_QC), col(REST_KC), col(REST_VC)],
        out_specs=pl.BlockSpec((1, l, w), lambda i: (i, 0, 0)),
        compiler_params=_cparams(("parallel",)),
        name="context_attention",
    )(p3, p3, p3)


def _attn_lat_kernel(q_ref, k_ref, v_ref, kc_ref, vc_ref, bias_ref, o_ref, *, rows):
    i = pl.program_id(1)
    nwin = NA_WROWS * GRID_W
    ws = jnp.clip(i * NA_QROWS - WIN_H // 2, 0, rows - NA_WROWS)
    start = pl.multiple_of(ws * GRID_W, GRID_W)
    q = q_ref[0] * (NA_DIM ** -0.5)
    window = (k_ref[0, pl.ds(start, nwin), :].astype(BF16), v_ref[0, pl.ds(start, nwin), :].astype(BF16),
              lambda h: bias_ref[0, h])
    context = (kc_ref[0].astype(BF16), vc_ref[0].astype(BF16), None)
    o_ref[0] = _softmax_pv(q, [window, context]).astype(BF16)


def neighbourhood_attention(p3, k_ctx, v_ctx, bias_tab):
    b, n, _ = p3.shape
    rows = n // GRID_W
    w = NA_HEADS * NA_DIM
    tq = NA_QROWS * GRID_W
    ni = rows // NA_QROWS
    nctx = k_ctx.shape[1]
    kern = functools.partial(_attn_lat_kernel, rows=rows)
    case = lambda i, t: (jnp.where(t == 0, 0, jnp.where(t == ni - 1, 2, 1)), 0, 0, 0)
    return pl.pallas_call(
        kern,
        out_shape=jax.ShapeDtypeStruct((b, n, w), BF16),
        grid=(b, ni),
        in_specs=[pl.BlockSpec((1, tq, w), lambda i, t: (i, t, REST_QC // w)),
                  pl.BlockSpec((1, n, w), lambda i, t: (i, 0, REST_KC // w)),
                  pl.BlockSpec((1, n, w), lambda i, t: (i, 0, REST_VC // w)),
                  pl.BlockSpec((1, nctx, w), lambda i, t: (i, 0, 0)),
                  pl.BlockSpec((1, nctx, w), lambda i, t: (i, 0, 0)),
                  pl.BlockSpec((1, NA_HEADS, tq, NA_WROWS * GRID_W), case)],
        out_specs=pl.BlockSpec((1, tq, w), lambda i, t: (i, t, 0)),
        compiler_params=_cparams(("parallel", "arbitrary"), VMEM_LIMIT),
        name="neighbourhood_attention",
    )(p3, p3, p3, k_ctx, v_ctx, bias_tab)


def _bias_kernel(rpb_ref, o_ref, *, rows):
    l = pl.program_id(0)
    h = pl.program_id(1)
    n_dr, n_dc = 2 * WIN_H - 1, 2 * WIN_W - 1
    base = (l * NA_HEADS + h) * (n_dr * n_dc)
    c = lax.broadcasted_iota(jnp.int32, (GRID_W, 2 * GRID_W), 0)
    lane = lax.broadcasted_iota(jnp.int32, (GRID_W, 2 * GRID_W), 1)
    kc = lane & (GRID_W - 1)
    dc = jnp.clip(kc - c + (WIN_W - 1), 0, n_dc - 1)
    cs = jnp.clip(c - WIN_W // 2, 0, GRID_W - WIN_W)
    col_ok = (kc >= cs) & (kc < cs + WIN_W)
    neg = jnp.full((GRID_W, 2 * GRID_W), NEG, F32)
    tabs = []
    for a in range(n_dr):
        t = jnp.zeros((GRID_W, 2 * GRID_W), F32)
        for bb in range(n_dc):
            t = jnp.where(dc == bb, rpb_ref[base + a * n_dc + bb], t)
        tabs.append(jnp.where(col_ok, t, NEG))
    interior = 2 * NA_QROWS
    cases = ((0, 0), (interior, interior - WIN_H // 2), (rows - NA_QROWS, rows - NA_WROWS))
    for case, (r0, ws) in enumerate(cases):
        for iq in range(NA_QROWS):
            r = r0 + iq
            rs = min(max(r - WIN_H // 2, 0), rows - WIN_H)
            for jp in range(NA_WROWS // 2):
                halves = []
                for half in range(2):
                    kr = ws + 2 * jp + half
                    halves.append(tabs[kr - r + WIN_H - 1] if rs <= kr < rs + WIN_H else neg)
                blk = jnp.where(lane < GRID_W, halves[0], halves[1])
                o_ref[0, case, 0, iq * GRID_W:(iq + 1) * GRID_W, jp * 2 * GRID_W:(jp + 1) * 2 * GRID_W] = blk


def attention_bias_tables(na_rpb, rows):
    depth = na_rpb.shape[0]
    kern = functools.partial(_bias_kernel, rows=rows)
    tq, nw = NA_QROWS * GRID_W, NA_WROWS * GRID_W
    return pl.pallas_call(
        kern,
        out_shape=jax.ShapeDtypeStruct((depth, 3, NA_HEADS, tq, nw), F32),
        grid=(depth, NA_HEADS),
        in_specs=[pl.BlockSpec(memory_space=pltpu.SMEM)],
        out_specs=pl.BlockSpec((1, 3, 1, tq, nw), lambda l, h: (l, 0, h, 0, 0)),
        compiler_params=_cparams(("arbitrary", "arbitrary")),
        name="attention_bias_tables",
    )(na_rpb.reshape(-1))


FFN_HALO = 16
FFN_CHUNK = 256


def _out_ffn_kernel(x_ref, xp_ref, xn_ref, oa_ref, oap_ref, oan_ref, of_ref, ofp_ref, ofn_ref,
                    ob_ref, obp_ref, obn_ref, z_ref, zp_ref, zn_ref, oc_ref, ocp_ref, ocn_ref,
                    wo_ref, wg_ref, wv_ref, wd_ref, cwg_ref, cwv_ref, cbg_ref, cbv_ref,
                    mod_ref, gpm_ref, gpf_ref, gnw_ref, gpo_ref, o_ref,
                    mix_scr, hext_scr, act_scr, *, row0, tiles_per_mod, tiles_per_seq, tm):
    i = pl.program_id(0)
    row = row0 + i // tiles_per_mod
    g1 = mod_ref[0, 2, pl.ds(row, 1), :]
    sh2 = mod_ref[0, 3, pl.ds(row, 1), :]
    sc2 = mod_ref[0, 4, pl.ds(row, 1), :]
    g2 = mod_ref[0, 5, pl.ds(row, 1), :]
    hh = FFN_HALO
    gw = GDN_HEADS * GDN_D
    parts = ((0, tm, x_ref, oa_ref, of_ref, ob_ref, z_ref, oc_ref),
             (tm, hh, xp_ref, oap_ref, ofp_ref, obp_ref, zp_ref, ocp_ref),
             (tm + hh, hh, xn_ref, oan_ref, ofn_ref, obn_ref, zn_ref, ocn_ref))
    for r0, n, _, a_ref, f_ref, b_ref, zz_ref, c_ref in parts:
        mix_scr[r0:r0 + n, 0:CONV_CH] = a_ref[...]
        og = f_ref[...] + b_ref[...]
        zz = zz_ref[...]
        for h in range(GDN_HEADS):
            sl = slice(h * GDN_D, (h + 1) * GDN_D)
            gated = _rms(og[:, sl], gnw_ref[...]) * _silu(zz[:, sl])
            mix_scr[r0:r0 + n, CONV_CH + h * GDN_D:CONV_CH + (h + 1) * GDN_D] = gated.astype(BF16)
        mix_scr[r0:r0 + n, CONV_CH + gw:] = c_ref[...]
    m = _dot(mix_scr[...], wo_ref[...])
    x1 = None
    for r0, n, xx_ref, *_ in parts:
        x1_part = xx_ref[...] + g1 * _rms(m[r0:r0 + n], gpm_ref[...])
        hext_scr[r0:r0 + n, :] = (_rms(x1_part, gpf_ref[...]) * (1.0 + sc2) + sh2).astype(BF16)
        if r0 == 0:
            x1 = x1_part
    t_in_seq = i % tiles_per_seq
    keep_prev = (t_in_seq > 0).astype(F32)
    keep_next = (t_in_seq < tiles_per_seq - 1).astype(F32)
    row8 = lax.broadcasted_iota(jnp.int32, (8, FFN_CHUNK), 0)

    def conv(u_ext, cw_ref, cb_ref, sl):
        u = u_ext[0:tm]
        prev_row = u_ext[tm + hh - 1:tm + hh] * keep_prev
        next_row = u_ext[tm + hh:tm + hh + 1] * keep_next
        up = pltpu.roll(u, 1, 0)
        dn = pltpu.roll(u, tm - 1, 0)
        up = jnp.concatenate([jnp.where(row8 == 0, prev_row, up[0:8]), up[8:]], axis=0)
        dn = jnp.concatenate([dn[:tm - 8], jnp.where(row8 == 7, next_row, dn[tm - 8:])], axis=0)
        return up * cw_ref[0:1, sl] + u * cw_ref[1:2, sl] + dn * cw_ref[2:3, sl] + cb_ref[:, sl]

    n_chunks = D_FF // FFN_CHUNK
    sls = [slice(c * FFN_CHUNK, (c + 1) * FFN_CHUNK) for c in range(n_chunks)]

    def up_proj(c):
        return _dot(hext_scr[...], wg_ref[:, sls[c]]), _dot(hext_scr[...], wv_ref[:, sls[c]])

    nxt = up_proj(0)
    for c in range(n_chunks):
        ug, uv = nxt
        if c + 1 < n_chunks:
            nxt = up_proj(c + 1)
        gate = conv(ug, cwg_ref, cbg_ref, sls[c])
        val = conv(uv, cwv_ref, cbv_ref, sls[c])
        act_scr[:, sls[c]] = (_silu(gate) * val).astype(BF16)
    f = _dot(act_scr[...], wd_ref[...])
    o_ref[...] = x1 + g2 * _rms(f, gpo_ref[...])


def out_ffn(x, oa, of, ob, rest, oc, w_out, wg, wv, wd, cwg, cwv, cbg, cbv, mod, layer, g_post_mix, g_pre_ffn, gnw,
            g_post_ffn, *, row0, per_seq, seq_len, tm):
    t = x.shape[0]
    assert seq_len % tm == 0 and per_seq % seq_len == 0
    gw = GDN_HEADS * GDN_D
    hb = tm // FFN_HALO
    last = t // FFN_HALO - 1
    kern = functools.partial(_out_ffn_kernel, row0=row0, tiles_per_mod=per_seq // tm, tiles_per_seq=seq_len // tm, tm=tm)

    def trio(n):
        return [pl.BlockSpec((tm, n), lambda i: (i, 0)),
                pl.BlockSpec((FFN_HALO, n), lambda i: (jnp.maximum(i * hb - 1, 0), 0)),
                pl.BlockSpec((FFN_HALO, n), lambda i: (jnp.minimum((i + 1) * hb, last), 0))]

    const = lambda shape: _const_spec(shape, lambda i: (0,) * len(shape))
    return pl.pallas_call(
        kern,
        out_shape=jax.ShapeDtypeStruct((t, D_MODEL), F32),
        grid=(t // tm,),
        in_specs=(trio(D_MODEL) + trio(CONV_CH) + trio(gw) + trio(gw) + trio(gw) + trio(NA_HEADS * NA_DIM)
                  + [const((D_MODEL, D_MODEL)), const((D_MODEL, D_FF)), const((D_MODEL, D_FF)), const((D_FF, D_MODEL)),
                     const((3, D_FF)), const((3, D_FF)), const((1, D_FF)), const((1, D_FF)),
                     _const_spec((1, 6, mod.shape[2], D_MODEL), lambda i: (layer, 0, 0, 0)),
                     const((1, D_MODEL)), const((1, D_MODEL)), const((1, GDN_D)), const((1, D_MODEL))]),
        out_specs=pl.BlockSpec((tm, D_MODEL), lambda i: (i, 0)),
        scratch_shapes=[pltpu.VMEM((tm + 2 * FFN_HALO, D_MODEL), BF16), pltpu.VMEM((tm + 2 * FFN_HALO, D_MODEL), BF16),
                        pltpu.VMEM((tm, D_FF), BF16)],
        compiler_params=_cparams(("parallel",), VMEM_LIMIT),
        name="out_ffn",
    )(x, x, x, oa, oa, oa, of, of, of, ob, ob, ob, rest, rest, rest, oc, oc, oc, w_out, wg, wv, wd, cwg, cwv, cbg, cbv,
      mod, g_post_mix, g_pre_ffn, gnw, g_post_ffn)


def _trunk_layer(x, lw, layer, mod, *, batch, seq, row0, per_seq, ctx):
    tm = min(512, seq)
    rest, oa, qn, kn, vn, bg, bgt = mix_in(
        x, mod, layer, lw["g_pre_mix"], lw["w_in"], lw["conv_w"], lw["conv_b"], lw["conv_ln_g"], lw["conv_ln_b"],
        lw["gdn_conv_w"], lw["alog_row"], lw["dtb_row"], row0=row0, per_seq=per_seq, seq_len=seq, tm=tm)
    gw = GDN_HEADS * GDN_D
    p3 = rest.reshape(batch, seq, REST_COLS)
    qn, kn, vn = (a.reshape(batch, seq, gw) for a in (qn, kn, vn))
    bg = bg.reshape(batch, seq, 128)
    if ctx is None:
        state0 = jnp.zeros((batch, 1, 2, GDN_HEADS, GDN_D, GDN_D), F32)
        of, ob, s_fin = gdn_scan(qn, kn, vn, bg, bgt, state0, 0)
        oc = context_attention(p3)
    else:
        k_ctx, v_ctx, bias_tab, state_delta = ctx
        of, ob, s_fin = gdn_scan(qn, kn, vn, bg, bgt, state_delta, layer)
        oc = neighbourhood_attention(p3, k_ctx, v_ctx, bias_tab)
    t = batch * seq
    x2 = out_ffn(x, oa, of.reshape(t, gw), ob.reshape(t, gw), rest, oc.reshape(t, NA_HEADS * NA_DIM), lw["w_out"],
                 lw["wg"], lw["wv"], lw["wd"], lw["cwg"], lw["cwv"], lw["cbg"], lw["cbv"], mod, layer,
                 lw["g_post_mix"], lw["g_pre_ffn"], lw["gdn_norm_w"], lw["g_post_ffn"],
                 row0=row0, per_seq=per_seq, seq_len=seq, tm=tm)
    return x2, p3, s_fin


def kernel(x_prompt, x_sample, cache_attn_kv, state_delta, c, c_ctx, w_ada, b_ada, g_pre_mix, g_post_mix, g_pre_ffn, g_post_ffn, w_in, w_out, conv_w, conv_b, conv_ln_g, conv_ln_b, gdn_conv_w, gdn_a_log, gdn_dt_bias, gdn_norm_w, na_rpb, w_up, ffn_conv_w, ffn_conv_b, w_down):
    depth = w_in.shape[0]
    bp, sp, d = x_prompt.shape
    bs, ss, _ = x_sample.shape
    rows = ss // GRID_W
    nmod = 16
    assert 1 + bs <= nmod

    n_gdn = 2 * GDN_HEADS
    c0 = 2 * CONV_CH + 4 * GDN_HEADS * GDN_D
    w_in_b = w_in.astype(BF16)
    w_in_r = jnp.concatenate(
        [w_in_b[:, :, :c0], w_in_b[:, :, c0 + 2 * n_gdn:], w_in_b[:, :, c0:c0 + 2 * n_gdn],
         jnp.zeros((depth, d, P_DIM - w_in.shape[2]), BF16)], axis=-1)
    w_out_b = w_out.astype(BF16)
    wg_b = w_up[:, :, :D_FF].astype(BF16)
    wv_b = w_up[:, :, D_FF:].astype(BF16)
    wd_b = w_down.astype(BF16)

    def lane_row(vals):
        return jnp.pad(vals.reshape(depth, 1, n_gdn), ((0, 0), (0, 0), (8, 128 - 8 - n_gdn)))

    alog_rows = lane_row(gdn_a_log)
    dtb_rows = lane_row(gdn_dt_bias)

    cvecs = jnp.concatenate([c_ctx[None, :], c, jnp.zeros((nmod - 1 - bs, d), F32)], axis=0)
    mod = ada_modulation(cvecs, w_ada, b_ada)
    bias_tabs = attention_bias_tables(na_rpb, rows)

    xp = x_prompt.reshape(bp * sp, d)
    xs = x_sample.reshape(bs * ss, d)
    kv_new, st_new = [], []
    hw = NA_HEADS * NA_DIM
    for l in range(depth):
        lw = {
            "g_pre_mix": g_pre_mix[l][None], "g_post_mix": g_post_mix[l][None],
            "g_pre_ffn": g_pre_ffn[l][None], "g_post_ffn": g_post_ffn[l][None],
            "w_in": w_in_r[l], "w_out": w_out_b[l],
            "conv_w": conv_w[l], "conv_b": conv_b[l][None], "conv_ln_g": conv_ln_g[l][None],
            "conv_ln_b": conv_ln_b[l][None],
            "gdn_conv_w": gdn_conv_w[l], "alog_row": alog_rows[l], "dtb_row": dtb_rows[l],
            "gdn_norm_w": gdn_norm_w[l][None],
            "wg": wg_b[l], "wv": wv_b[l], "wd": wd_b[l],
            "cwg": ffn_conv_w[l][:, :D_FF], "cwv": ffn_conv_w[l][:, D_FF:],
            "cbg": ffn_conv_b[l][None, :D_FF], "cbv": ffn_conv_b[l][None, D_FF:],
        }
        xp, p3, s_fin = _trunk_layer(xp, lw, l, mod, batch=bp, seq=sp, row0=0, per_seq=bp * sp, ctx=None)
        heads = lambda col: p3[:, :, col:col + hw].reshape(bp, sp, NA_HEADS, NA_DIM).transpose(0, 2, 1, 3)
        kv_new.append(jnp.stack([heads(REST_KC), heads(REST_VC)], axis=1))
        st_new.append(s_fin)
        lanes = lambda t: t.transpose(0, 2, 1, 3).reshape(bs, t.shape[2], hw)
        ctx = (lanes(cache_attn_kv[:, l, 0]), lanes(cache_attn_kv[:, l, 1]), bias_tabs[l], state_delta)
        xs, _, _ = _trunk_layer(xs, lw, l, mod, batch=bs, seq=ss, row0=1, per_seq=ss, ctx=ctx)
    return (xp.reshape(bp, sp, d), xs.reshape(bs, ss, d), jnp.stack(kv_new, axis=1), jnp.stack(st_new, axis=1))
```

```python
import functools
import math

import jax
import jax.numpy as jnp
from jax import lax
from jax.experimental import pallas as pl
from jax.experimental.pallas import tpu as pltpu

F32 = jnp.float32
BF16 = jnp.bfloat16
EPS = 1e-6
NEG = -1e30

D_MODEL = 1024
CONV_CH = 256
CONV_K = 31
GDN_HEADS = 4
GDN_D = 128
NA_HEADS = 4
NA_DIM = 64
GRID_W = 64
WIN_H = 8
WIN_W = 16
D_FF = 2816
CHUNK = 64
GDN_BLOCK = 1024
GDN_SUB = 256
NA_QROWS = 4
NA_WROWS = 12

COL_AV, COL_AG = 0, 256
COL_Q, COL_K, COL_V, COL_Z = 512, 1024, 1536, 2048
COL_QC, COL_KC, COL_VC = 2560, 2816, 3072
COL_BD = 3328
P_DIM = 3456

VMEM_LIMIT = 56 * 1024 * 1024


def _cparams(sem, vmem=None):
    return pltpu.CompilerParams(dimension_semantics=sem, vmem_limit_bytes=vmem)


def _const_spec(shape, index_map):
    return pl.BlockSpec(shape, index_map, pipeline_mode=pl.Buffered(1))


def _silu(x):
    return x * jax.nn.sigmoid(x)


def _rms(x, g):
    return x * lax.rsqrt(jnp.mean(x * x, axis=-1, keepdims=True) + EPS) * g


def _dot(a, b):
    return jnp.dot(a, b, preferred_element_type=F32)


def _dot_nt(a, b):
    return lax.dot_general(a, b, (((1,), (1,)), ((), ())), preferred_element_type=F32)


def _dot_tn(a, b):
    return lax.dot_general(a, b, (((0,), (0,)), ((), ())), preferred_element_type=F32)


def _mm(a, b):
    return _dot(a.astype(BF16), b.astype(BF16))


def _half_rows(x, s, half):
    n, c = x.shape[0] // (2 * s), x.shape[1]
    return x.reshape(n, 2, s, c)[:, half].reshape(n * s, c)


def _expand_rows(y, s, half):
    n, c = y.shape[0] // s, y.shape[1]
    y4 = y.reshape(n, 1, s, c)
    z = jnp.zeros_like(y4)
    return jnp.concatenate([z, y4] if half == 1 else [y4, z], axis=1).reshape(2 * n * s, c)


def _ada_kernel(c_ref, w_ref, b_ref, o_ref):
    s = _silu(c_ref[...]).astype(BF16)
    o_ref[0, 0] = _dot(s, w_ref[0].astype(BF16)) + b_ref[0, 0]


def ada_modulation(cvecs, w_ada, b_ada):
    depth = w_ada.shape[0]
    n = cvecs.shape[0]
    b4 = b_ada.reshape(depth, 6, 1, D_MODEL)
    return pl.pallas_call(
        _ada_kernel,
        out_shape=jax.ShapeDtypeStruct((depth, 6, n, D_MODEL), F32),
        grid=(depth, 6),
        in_specs=[pl.BlockSpec((n, D_MODEL), lambda l, j: (0, 0)),
                  pl.BlockSpec((1, D_MODEL, D_MODEL), lambda l, j: (l, 0, j)),
                  pl.BlockSpec((1, 1, 1, D_MODEL), lambda l, j: (l, j, 0, 0))],
        out_specs=pl.BlockSpec((1, 1, n, D_MODEL), lambda l, j: (l, j, 0, 0)),
        compiler_params=_cparams(("arbitrary", "arbitrary")),
        name="ada_modulation",
    )(cvecs, w_ada, b4)


MIX_HALO = 16
CONF_ROWS = 64
BG_ROWS = 24
REST_COLS = COL_BD - COL_Z
REST_Z, REST_QC, REST_KC, REST_VC = 0, COL_QC - COL_Z, COL_KC - COL_Z, COL_VC - COL_Z


def _mix_in_kernel(x_ref, xp_ref, xn_ref, mod_ref, g_ref, w_ref, cw_ref, cb_ref, lg_ref, lb_ref, gcw_ref,
                   alog_ref, dtb_ref, rest_ref, oa_ref, qo_ref, ko_ref, vo_ref, bg_ref, bgt_ref,
                   hext_scr, u_scr, rot_scr, *, row0, tiles_per_mod, tiles_per_seq, tm):
    i = pl.program_id(0)
    row = row0 + i // tiles_per_mod
    t_in_seq = i % tiles_per_seq
    keep_prev = (t_in_seq > 0).astype(F32)
    keep_next = (t_in_seq < tiles_per_seq - 1).astype(F32)
    sh = mod_ref[0, 0, pl.ds(row, 1), :]
    sc = mod_ref[0, 1, pl.ds(row, 1), :]
    hh = MIX_HALO
    gw = GDN_HEADS * GDN_D

    def norm(x):
        return (_rms(x, g_ref[...]) * (1.0 + sc) + sh).astype(BF16)

    hext_scr[0:hh, :] = norm(xp_ref[...])
    hext_scr[hh:hh + tm, :] = norm(x_ref[...])
    hext_scr[hh + tm:2 * hh + tm, :] = norm(xn_ref[...])

    def proj_ext(col, width):
        return _dot(hext_scr[...], w_ref[:, col:col + width])

    def proj(col, width):
        return _dot(hext_scr[hh:hh + tm, :], w_ref[:, col:col + width])

    pa = proj_ext(COL_AV, 2 * CONV_CH)
    pq = proj_ext(COL_Q, gw)
    u = pa[:, :CONV_CH] * jax.nn.sigmoid(pa[:, CONV_CH:])
    u_scr[0:hh, :] = u[0:hh] * keep_prev
    u_scr[hh:hh + tm, :] = u[hh:hh + tm]
    u_scr[hh + tm:2 * hh + tm, :] = u[hh + tm:] * keep_next
    n_ext = tm + 2 * hh
    u_ext = u_scr[...]
    for r in range(1, 8):
        rot_scr[r - 1] = pltpu.roll(u_ext, n_ext - r, 0)
    pk = proj_ext(COL_K, gw)
    half = CONV_K // 2
    for r0 in range(0, tm, CONF_ROWS):
        acc = jnp.zeros((CONF_ROWS, CONV_CH), F32) + cb_ref[...]
        for k in range(CONV_K):
            off = r0 + k + hh - half
            r = off % 8
            src = u_scr[off:off + CONF_ROWS, :] if r == 0 else rot_scr[r - 1, off - r:off - r + CONF_ROWS, :]
            acc = acc + cw_ref[pl.ds(k, 1), :] * src
        mu = jnp.mean(acc, axis=-1, keepdims=True)
        cen = acc - mu
        var = jnp.mean(cen * cen, axis=-1, keepdims=True)
        y = cen * lax.rsqrt(var + EPS) * lg_ref[...] + lb_ref[...]
        oa_ref[r0:r0 + CONF_ROWS, :] = _silu(y).astype(BF16)

    row8 = lax.broadcasted_iota(jnp.int32, (8, gw), 0)

    def conv_silu(pe, j):
        x = pe[hh:hh + tm]
        prev_row = pe[hh - 1:hh] * keep_prev
        next_row = pe[hh + tm:hh + tm + 1] * keep_next
        up = pltpu.roll(x, 1, 0)
        dn = pltpu.roll(x, tm - 1, 0)
        up = jnp.concatenate([jnp.where(row8 == 0, prev_row, up[0:8]), up[8:]], axis=0)
        dn = jnp.concatenate([dn[:tm - 8], jnp.where(row8 == 7, next_row, dn[tm - 8:])], axis=0)
        w = gcw_ref[:, j * gw:(j + 1) * gw]
        return _silu(up * w[0:1, :] + x * w[1:2, :] + dn * w[2:3, :])

    def l2n(seg, scale):
        return seg * (lax.rsqrt(jnp.sum(seg * seg, axis=-1, keepdims=True) + EPS) * scale)

    pv = proj_ext(COL_V, gw)
    q = conv_silu(pq, 0)
    for h in range(GDN_HEADS):
        sl = slice(h * GDN_D, (h + 1) * GDN_D)
        qo_ref[:, sl] = l2n(q[:, sl], GDN_D ** -0.5)
    rest_ref[:, 0:gw] = proj(COL_Z, gw)
    k = conv_silu(pk, 1)
    for h in range(GDN_HEADS):
        sl = slice(h * GDN_D, (h + 1) * GDN_D)
        ko_ref[:, sl] = l2n(k[:, sl], 1.0)
    rest_ref[:, gw:2 * gw] = proj(COL_QC, gw)
    vo_ref[...] = conv_silu(pv, 2)
    tail = proj(COL_VC, P_DIM - COL_VC)
    rest_ref[:, 2 * gw:REST_COLS] = tail[:, :REST_COLS - 2 * gw]

    bd = tail[:, REST_COLS - 2 * gw:]
    lane = lax.broadcasted_iota(jnp.int32, (tm, 128), 1)
    rc = lax.broadcasted_iota(jnp.int32, (tm, 128), 0) & (CHUNK - 1)
    is_g = (lane >= 8) & (lane < 16)
    beta = jax.nn.sigmoid(bd)
    xg = bd + dtb_ref[...]
    softplus = jnp.maximum(xg, 0.0) + jnp.log1p(jnp.exp(-jnp.abs(xg)))
    g = jnp.where(is_g, -jnp.exp(alog_ref[...]) * softplus, 0.0)
    pre = g
    suf = g
    s = 1
    while s < CHUNK:
        pre = pre + jnp.where(rc >= s, pltpu.roll(pre, s, 0), 0.0)
        suf = suf + jnp.where(rc < CHUNK - s, pltpu.roll(suf, tm - s, 0), 0.0)
        s *= 2
    tot = pre + suf - g
    bg = jnp.where(lane < 8, beta,
                   jnp.where(lane < 12, pre,
                             jnp.where(lane < 16, suf, pltpu.roll(tot, 8, 1))))
    bg = jnp.where(lane < 24, bg, 0.0)
    bg_ref[...] = bg
    bgt_ref[0] = jnp.transpose(bg)[0:BG_ROWS, :]


def mix_in(x, mod, layer, g_pre, w_in, conv_w, conv_b, ln_g, ln_b, gdn_conv_w, alog_row, dtb_row, *,
           row0, per_seq, seq_len, tm):
    t = x.shape[0]
    assert seq_len % tm == 0 and per_seq % seq_len == 0 and tm % CONF_ROWS == 0 and tm % CHUNK == 0
    gw = GDN_HEADS * GDN_D
    hb = tm // MIX_HALO
    last = t // MIX_HALO - 1
    tps = seq_len // tm
    kern = functools.partial(_mix_in_kernel, row0=row0, tiles_per_mod=per_seq // tm, tiles_per_seq=tps, tm=tm)
    rowblk = lambda n: pl.BlockSpec((tm, n), lambda i: (i, 0))
    const = lambda shape: _const_spec(shape, lambda i: (0,) * len(shape))
    seq = jax.ShapeDtypeStruct((t, gw), F32)
    return pl.pallas_call(
        kern,
        out_shape=(jax.ShapeDtypeStruct((t, REST_COLS), F32), jax.ShapeDtypeStruct((t, CONV_CH), BF16),
                   seq, seq, seq, jax.ShapeDtypeStruct((t, 128), F32),
                   jax.ShapeDtypeStruct((t // seq_len, BG_ROWS, seq_len), F32)),
        grid=(t // tm,),
        in_specs=[rowblk(D_MODEL),
                  pl.BlockSpec((MIX_HALO, D_MODEL), lambda i: (jnp.maximum(i * hb - 1, 0), 0)),
                  pl.BlockSpec((MIX_HALO, D_MODEL), lambda i: (jnp.minimum((i + 1) * hb, last), 0)),
                  _const_spec((1, 6, mod.shape[2], D_MODEL), lambda i: (layer, 0, 0, 0)),
                  const((1, D_MODEL)), const((D_MODEL, P_DIM)),
                  const((CONV_K, CONV_CH)), const((1, CONV_CH)), const((1, CONV_CH)), const((1, CONV_CH)),
                  const((3, 3 * gw)), const((1, 128)), const((1, 128))],
        out_specs=(rowblk(REST_COLS), rowblk(CONV_CH), rowblk(gw), rowblk(gw), rowblk(gw), rowblk(128),
                   pl.BlockSpec((1, BG_ROWS, tm), lambda i: (i // tps, 0, i % tps))),
        scratch_shapes=[pltpu.VMEM((tm + 2 * MIX_HALO, D_MODEL), BF16),
                        pltpu.VMEM((tm + 2 * MIX_HALO, CONV_CH), F32),
                        pltpu.VMEM((7, tm + 2 * MIX_HALO, CONV_CH), F32)],
        compiler_params=_cparams(("parallel",), VMEM_LIMIT),
        name="mix_in",
    )(x, x, x, mod, g_pre, w_in, conv_w, conv_b, ln_g, ln_b, gdn_conv_w, alog_row, dtb_row)


def _gscan_kernel(qf_ref, kf_ref, vf_ref, bgf_ref, bgtf_ref, qb_ref, kb_ref, vb_ref, bgb_ref, bgtb_ref,
                  s0_ref, of_ref, ob_ref, sfin_ref, s_scr, vn_scr, *, r):
    nb = pl.program_id(1)
    sub = min(GDN_SUB, r)
    nsub = r // sub
    nchunk = sub // CHUNK
    shift = int(math.log2(CHUNK))

    @pl.when(nb == 0)
    def _():
        s_scr[...] = s0_ref[0, 0]

    vn_scr[...] = jnp.zeros((2 * GDN_HEADS, sub, GDN_D), F32)
    ii = lax.broadcasted_iota(jnp.int32, (sub, sub), 0)
    jj = lax.broadcasted_iota(jnp.int32, (sub, sub), 1)
    blk_same = {s: (ii >> s) == (jj >> s) for s in range(3, shift + 1)}
    same = blk_same[shift]
    eye = (ii == jj).astype(F32)
    dirs = ((qf_ref, kf_ref, vf_ref, bgf_ref, bgtf_ref, of_ref), (qb_ref, kb_ref, vb_ref, bgb_ref, bgtb_ref, ob_ref))
    masks = ((same & (ii >= jj), same & (ii > jj)), (same & (ii <= jj), same & (ii < jj)))
    sl_h = [slice(h * GDN_D, (h + 1) * GDN_D) for h in range(GDN_HEADS)]
    nblk = 2 * GDN_HEADS

    def prepare(group, out):
        rs = [slice(sb * sub, (sb + 1) * sub) for sb, d, h in group]

        def each(fn):
            return [fn(i, group[i][1], group[i][2]) for i in range(nblk)]

        def col(off):
            return each(lambda i, d, h: dirs[d][3][0, rs[i], off + d * GDN_HEADS + h:off + d * GDN_HEADS + h + 1])

        beta, gcc, gl = col(0), col(8), col(16)
        gcr = each(lambda i, d, h: dirs[d][4][0, 8 + d * GDN_HEADS + h:9 + d * GDN_HEADS + h, rs[i]])
        q = each(lambda i, d, h: dirs[d][0][0, rs[i], sl_h[h]])
        k = each(lambda i, d, h: dirs[d][1][0, rs[i], sl_h[h]])
        v = each(lambda i, d, h: dirs[d][2][0, rs[i], sl_h[h]])
        decay = each(lambda i, d, h: jnp.exp(jnp.where(masks[d][0], gcc[i] - gcr[i], NEG)))
        kbeta = each(lambda i, d, h: k[i] * beta[i])
        kb16 = each(lambda i, d, h: k[i].astype(BF16))
        yield
        lm = each(lambda i, d, h: jnp.where(masks[d][1], _dot_nt(kbeta[i].astype(BF16), kb16[i]) * decay[i], 0.0))
        egc = each(lambda i, d, h: jnp.exp(gcc[i]))
        x = each(lambda i, d, h: jnp.concatenate([v[i] * beta[i], kbeta[i] * egc[i]], axis=1))
        yield
        l8 = each(lambda i, d, h: jnp.where(blk_same[3], lm[i], 0.0))
        m2 = each(lambda i, d, h: _mm(l8[i], l8[i]))
        yield
        m4 = each(lambda i, d, h: _mm(m2[i], m2[i]))
        t = each(lambda i, d, h: eye - l8[i])
        t = each(lambda i, d, h: t[i] + _mm(t[i], m2[i]))
        yield
        t = each(lambda i, d, h: t[i] + _mm(t[i], m4[i]))
        yield
        for lvl in range(3, shift):
            s_rows = 1 << lvl
            off = blk_same[lvl + 1] & jnp.logical_not(blk_same[lvl])
            ct = each(lambda i, d, h: _mm(_half_rows(jnp.where(off, lm[i], 0.0), s_rows, 1 - d), t[i]))
            yield
            upd = each(lambda i, d, h: _mm(_half_rows(t[i], s_rows, 1 - d), _expand_rows(ct[i], s_rows, 1 - d)))
            t = each(lambda i, d, h: t[i] - _expand_rows(upd[i], s_rows, 1 - d))
            yield
        out["x"] = each(lambda i, d, h: _mm(t[i], x[i]))
        yield
        out["intra"] = each(lambda i, d, h: (_dot_nt(q[i].astype(BF16), kb16[i]) * decay[i]).astype(BF16))
        out["qd"] = each(lambda i, d, h: q[i] * egc[i])
        out["kd"] = each(lambda i, d, h: (k[i] * jnp.exp(gl[i] - gcc[i])).astype(BF16))
        out["eg_last"] = each(lambda i, d, h: jnp.exp(jnp.broadcast_to(gl[i], (sub, GDN_D))))
        yield

    def scan(group, pre, state):
        x, qd, kd, intra, eg_last = pre["x"], pre["qd"], pre["kd"], pre["intra"], pre["eg_last"]
        outs = [[None] * nchunk for _ in range(nblk)]
        for step in range(nchunk):
            cs = [step if d == 0 else nchunk - 1 - step for sb, d, h in group]
            sls = [slice(c * CHUNK, (c + 1) * CHUNK) for c in cs]
            a = [_dot(jnp.concatenate([x[i][sls[i], GDN_D:], qd[i][sls[i]]], axis=0).astype(BF16),
                      state[group[i][1:]].astype(BF16)) for i in range(nblk)]
            v_new = [x[i][sls[i], :GDN_D] - a[i][:CHUNK] for i in range(nblk)]
            for i in range(nblk):
                vn_scr[i, sls[i], :] = v_new[i]
            yield
            for i in range(nblk):
                outs[i][cs[i]] = a[i][CHUNK:] + _dot(intra[i][sls[i], :], vn_scr[i].astype(BF16))
            for i in range(nblk):
                key = group[i][1:]
                state[key] = (state[key] * eg_last[i][cs[i] * CHUNK:cs[i] * CHUNK + 1, :]
                              + _dot_tn(kd[i][sls[i]], v_new[i].astype(BF16)))
            yield
        for i, (sb, d, h) in enumerate(group):
            dirs[d][5][0, sb * sub:(sb + 1) * sub, sl_h[h]] = jnp.concatenate(outs[i], axis=0)

    def run(*gens):
        gens = [g for g in gens if g is not None]
        while gens:
            for g in list(gens):
                try:
                    next(g)
                except StopIteration:
                    gens.remove(g)

    groups = [[(p, 0, h) for h in range(GDN_HEADS)] + [(nsub - 1 - p, 1, h) for h in range(GDN_HEADS)]
              for p in range(nsub)]
    state = {(d, h): s_scr[d, h] for d in range(2) for h in range(GDN_HEADS)}
    prev = None
    for p in range(nsub):
        pre = {}
        run(prepare(groups[p], pre), scan(groups[p - 1], prev, state) if p else None)
        prev = pre
    run(scan(groups[nsub - 1], prev, state))
    for (d, h), val in state.items():
        s_scr[d, h] = val
    sfin_ref[0] = s_scr[...]


def gdn_scan(q, k, v, bg, bgt, state0, layer):
    b, l, gw = q.shape
    r = min(GDN_BLOCK, l)
    nbk = l // r
    kern = functools.partial(_gscan_kernel, r=r)
    fwd = lambda i, n: (i, n, 0)
    bwd = lambda i, n: (i, nbk - 1 - n, 0)
    fwd_t = lambda i, n: (i, 0, n)
    bwd_t = lambda i, n: (i, 0, nbk - 1 - n)
    seq = lambda im: pl.BlockSpec((1, r, gw), im)
    st_shape = (2, GDN_HEADS, GDN_D, GDN_D)
    return pl.pallas_call(
        kern,
        out_shape=(jax.ShapeDtypeStruct((b, l, gw), F32), jax.ShapeDtypeStruct((b, l, gw), F32),
                   jax.ShapeDtypeStruct((b,) + st_shape, F32)),
        grid=(b, nbk),
        in_specs=[seq(fwd), seq(fwd), seq(fwd), pl.BlockSpec((1, r, 128), fwd), pl.BlockSpec((1, BG_ROWS, r), fwd_t),
                  seq(bwd), seq(bwd), seq(bwd), pl.BlockSpec((1, r, 128), bwd), pl.BlockSpec((1, BG_ROWS, r), bwd_t),
                  pl.BlockSpec((1, 1) + st_shape, lambda i, n: (i, layer, 0, 0, 0, 0))],
        out_specs=(seq(fwd), seq(bwd), pl.BlockSpec((1,) + st_shape, lambda i, n: (i, 0, 0, 0, 0))),
        scratch_shapes=[pltpu.VMEM(st_shape, F32), pltpu.VMEM((2 * GDN_HEADS, min(GDN_SUB, r), GDN_D), F32)],
        compiler_params=_cparams(("parallel", "arbitrary"), VMEM_LIMIT),
        name="gdn_scan",
    )(q, k, v, bg, bgt, q, k, v, bg, bgt, state0)


def _softmax_pv(q, parts):
    heads = range(NA_HEADS)
    lane = lax.broadcasted_iota(jnp.int32, (1, NA_HEADS * NA_DIM), 1)
    hm = [(lane >= h * NA_DIM) & (lane < (h + 1) * NA_DIM) for h in heads]
    qh = [jnp.where(hm[h], q, 0.0).astype(BF16) for h in heads]
    s = [[_dot_nt(qh[h], keys) if bias is None else _dot_nt(qh[h], keys) + bias(h) for keys, _, bias in parts]
         for h in heads]
    mx = [functools.reduce(jnp.maximum, [jnp.max(sp, axis=-1, keepdims=True) for sp in s[h]]) for h in heads]
    e = [[jnp.exp(sp - mx[h]) for sp in s[h]] for h in heads]
    inv = [1.0 / sum(jnp.sum(ep, axis=-1, keepdims=True) for ep in e[h]) for h in heads]
    o = [sum(_dot(ep.astype(BF16), vals) for ep, (_, vals, _) in zip(e[h], parts)) * inv[h] for h in heads]
    out = o[NA_HEADS - 1]
    for h in range(NA_HEADS - 2, -1, -1):
        out = jnp.where(hm[h], o[h], out)
    return out


def _attn_ctx_kernel(q_ref, k_ref, v_ref, o_ref):
    q = q_ref[0] * (NA_DIM ** -0.5)
    o_ref[0] = _softmax_pv(q, [(k_ref[0].astype(BF16), v_ref[0].astype(BF16), None)]).astype(BF16)


def context_attention(p3):
    b, l, _ = p3.shape
    w = NA_HEADS * NA_DIM
    col = lambda c: pl.BlockSpec((1, l, w), lambda i: (i, 0, c // w))
    return pl.pallas_call(
        _attn_ctx_kernel,
        out_shape=jax.ShapeDtypeStruct((b, l, w), BF16),
        grid=(b,),
        in_specs=[col(REST_QC), col(REST_KC), col(REST_VC)],
        out_specs=pl.BlockSpec((1, l, w), lambda i: (i, 0, 0)),
        compiler_params=_cparams(("parallel",)),
        name="context_attention",
    )(p3, p3, p3)


def _attn_lat_kernel(q_ref, k_ref, v_ref, kc_ref, vc_ref, bias_ref, o_ref, *, rows):
    i = pl.program_id(1)
    nwin = NA_WROWS * GRID_W
    ws = jnp.clip(i * NA_QROWS - WIN_H // 2, 0, rows - NA_WROWS)
    start = pl.multiple_of(ws * GRID_W, GRID_W)
    q = q_ref[0] * (NA_DIM ** -0.5)
    window = (k_ref[0, pl.ds(start, nwin), :].astype(BF16), v_ref[0, pl.ds(start, nwin), :].astype(BF16),
              lambda h: bias_ref[0, h])
    context = (kc_ref[0].astype(BF16), vc_ref[0].astype(BF16), None)
    o_ref[0] = _softmax_pv(q, [window, context]).astype(BF16)


def neighbourhood_attention(p3, k_ctx, v_ctx, bias_tab):
    b, n, _ = p3.shape
    rows = n // GRID_W
    w = NA_HEADS * NA_DIM
    tq = NA_QROWS * GRID_W
    ni = rows // NA_QROWS
    nctx = k_ctx.shape[1]
    kern = functools.partial(_attn_lat_kernel, rows=rows)
    case = lambda i, t: (jnp.where(t == 0, 0, jnp.where(t == ni - 1, 2, 1)), 0, 0, 0)
    return pl.pallas_call(
        kern,
        out_shape=jax.ShapeDtypeStruct((b, n, w), BF16),
        grid=(b, ni),
        in_specs=[pl.BlockSpec((1, tq, w), lambda i, t: (i, t, REST_QC // w)),
                  pl.BlockSpec((1, n, w), lambda i, t: (i, 0, REST_KC // w)),
                  pl.BlockSpec((1, n, w), lambda i, t: (i, 0, REST_VC // w)),
                  pl.BlockSpec((1, nctx, w), lambda i, t: (i, 0, 0)),
                  pl.BlockSpec((1, nctx, w), lambda i, t: (i, 0, 0)),
                  pl.BlockSpec((1, NA_HEADS, tq, NA_WROWS * GRID_W), case)],
        out_specs=pl.BlockSpec((1, tq, w), lambda i, t: (i, t, 0)),
        compiler_params=_cparams(("parallel", "arbitrary"), VMEM_LIMIT),
        name="neighbourhood_attention",
    )(p3, p3, p3, k_ctx, v_ctx, bias_tab)


def _bias_kernel(rpb_ref, o_ref, *, rows):
    l = pl.program_id(0)
    h = pl.program_id(1)
    n_dr, n_dc = 2 * WIN_H - 1, 2 * WIN_W - 1
    base = (l * NA_HEADS + h) * (n_dr * n_dc)
    c = lax.broadcasted_iota(jnp.int32, (GRID_W, 2 * GRID_W), 0)
    lane = lax.broadcasted_iota(jnp.int32, (GRID_W, 2 * GRID_W), 1)
    kc = lane & (GRID_W - 1)
    dc = jnp.clip(kc - c + (WIN_W - 1), 0, n_dc - 1)
    cs = jnp.clip(c - WIN_W // 2, 0, GRID_W - WIN_W)
    col_ok = (kc >= cs) & (kc < cs + WIN_W)
    neg = jnp.full((GRID_W, 2 * GRID_W), NEG, F32)
    tabs = []
    for a in range(n_dr):
        t = jnp.zeros((GRID_W, 2 * GRID_W), F32)
        for bb in range(n_dc):
            t = jnp.where(dc == bb, rpb_ref[base + a * n_dc + bb], t)
        tabs.append(jnp.where(col_ok, t, NEG))
    interior = 2 * NA_QROWS
    cases = ((0, 0), (interior, interior - WIN_H // 2), (rows - NA_QROWS, rows - NA_WROWS))
    for case, (r0, ws) in enumerate(cases):
        for iq in range(NA_QROWS):
            r = r0 + iq
            rs = min(max(r - WIN_H // 2, 0), rows - WIN_H)
            for jp in range(NA_WROWS // 2):
                halves = []
                for half in range(2):
                    kr = ws + 2 * jp + half
                    halves.append(tabs[kr - r + WIN_H - 1] if rs <= kr < rs + WIN_H else neg)
                blk = jnp.where(lane < GRID_W, halves[0], halves[1])
                o_ref[0, case, 0, iq * GRID_W:(iq + 1) * GRID_W, jp * 2 * GRID_W:(jp + 1) * 2 * GRID_W] = blk


def attention_bias_tables(na_rpb, rows):
    depth = na_rpb.shape[0]
    kern = functools.partial(_bias_kernel, rows=rows)
    tq, nw = NA_QROWS * GRID_W, NA_WROWS * GRID_W
    return pl.pallas_call(
        kern,
        out_shape=jax.ShapeDtypeStruct((depth, 3, NA_HEADS, tq, nw), F32),
        grid=(depth, NA_HEADS),
        in_specs=[pl.BlockSpec(memory_space=pltpu.SMEM)],
        out_specs=pl.BlockSpec((1, 3, 1, tq, nw), lambda l, h: (l, 0, h, 0, 0)),
        compiler_params=_cparams(("arbitrary", "arbitrary")),
        name="attention_bias_tables",
    )(na_rpb.reshape(-1))


FFN_HALO = 16
FFN_CHUNK = 256


def _out_ffn_kernel(x_ref, xp_ref, xn_ref, oa_ref, oap_ref, oan_ref, of_ref, ofp_ref, ofn_ref,
                    ob_ref, obp_ref, obn_ref, z_ref, zp_ref, zn_ref, oc_ref, ocp_ref, ocn_ref,
                    wo_ref, wg_ref, wv_ref, wd_ref, cwg_ref, cwv_ref, cbg_ref, cbv_ref,
                    mod_ref, gpm_ref, gpf_ref, gnw_ref, gpo_ref, o_ref,
                    mix_scr, hext_scr, act_scr, *, row0, tiles_per_mod, tiles_per_seq, tm):
    i = pl.program_id(0)
    row = row0 + i // tiles_per_mod
    g1 = mod_ref[0, 2, pl.ds(row, 1), :]
    sh2 = mod_ref[0, 3, pl.ds(row, 1), :]
    sc2 = mod_ref[0, 4, pl.ds(row, 1), :]
    g2 = mod_ref[0, 5, pl.ds(row, 1), :]
    hh = FFN_HALO
    gw = GDN_HEADS * GDN_D
    parts = ((0, tm, x_ref, oa_ref, of_ref, ob_ref, z_ref, oc_ref),
             (tm, hh, xp_ref, oap_ref, ofp_ref, obp_ref, zp_ref, ocp_ref),
             (tm + hh, hh, xn_ref, oan_ref, ofn_ref, obn_ref, zn_ref, ocn_ref))
    for r0, n, _, a_ref, f_ref, b_ref, zz_ref, c_ref in parts:
        mix_scr[r0:r0 + n, 0:CONV_CH] = a_ref[...]
        og = f_ref[...] + b_ref[...]
        zz = zz_ref[...]
        for h in range(GDN_HEADS):
            sl = slice(h * GDN_D, (h + 1) * GDN_D)
            gated = _rms(og[:, sl], gnw_ref[...]) * _silu(zz[:, sl])
            mix_scr[r0:r0 + n, CONV_CH + h * GDN_D:CONV_CH + (h + 1) * GDN_D] = gated.astype(BF16)
        mix_scr[r0:r0 + n, CONV_CH + gw:] = c_ref[...]
    m = _dot(mix_scr[...], wo_ref[...])
    x1 = None
    for r0, n, xx_ref, *_ in parts:
        x1_part = xx_ref[...] + g1 * _rms(m[r0:r0 + n], gpm_ref[...])
        hext_scr[r0:r0 + n, :] = (_rms(x1_part, gpf_ref[...]) * (1.0 + sc2) + sh2).astype(BF16)
        if r0 == 0:
            x1 = x1_part
    t_in_seq = i % tiles_per_seq
    keep_prev = (t_in_seq > 0).astype(F32)
    keep_next = (t_in_seq < tiles_per_seq - 1).astype(F32)
    row8 = lax.broadcasted_iota(jnp.int32, (8, FFN_CHUNK), 0)

    def conv(u_ext, cw_ref, cb_ref, sl):
        u = u_ext[0:tm]
        prev_row = u_ext[tm + hh - 1:tm + hh] * keep_prev
        next_row = u_ext[tm + hh:tm + hh + 1] * keep_next
        up = pltpu.roll(u, 1, 0)
        dn = pltpu.roll(u, tm - 1, 0)
        up = jnp.concatenate([jnp.where(row8 == 0, prev_row, up[0:8]), up[8:]], axis=0)
        dn = jnp.concatenate([dn[:tm - 8], jnp.where(row8 == 7, next_row, dn[tm - 8:])], axis=0)
        return up * cw_ref[0:1, sl] + u * cw_ref[1:2, sl] + dn * cw_ref[2:3, sl] + cb_ref[:, sl]

    n_chunks = D_FF // FFN_CHUNK
    sls = [slice(c * FFN_CHUNK, (c + 1) * FFN_CHUNK) for c in range(n_chunks)]

    def up_proj(c):
        return _dot(hext_scr[...], wg_ref[:, sls[c]]), _dot(hext_scr[...], wv_ref[:, sls[c]])

    nxt = up_proj(0)
    for c in range(n_chunks):
        ug, uv = nxt
        if c + 1 < n_chunks:
            nxt = up_proj(c + 1)
        gate = conv(ug, cwg_ref, cbg_ref, sls[c])
        val = conv(uv, cwv_ref, cbv_ref, sls[c])
        act_scr[:, sls[c]] = (_silu(gate) * val).astype(BF16)
    f = _dot(act_scr[...], wd_ref[...])
    o_ref[...] = x1 + g2 * _rms(f, gpo_ref[...])


def out_ffn(x, oa, of, ob, rest, oc, w_out, wg, wv, wd, cwg, cwv, cbg, cbv, mod, layer, g_post_mix, g_pre_ffn, gnw,
            g_post_ffn, *, row0, per_seq, seq_len, tm):
    t = x.shape[0]
    assert seq_len % tm == 0 and per_seq % seq_len == 0
    gw = GDN_HEADS * GDN_D
    hb = tm // FFN_HALO
    last = t // FFN_HALO - 1
    kern = functools.partial(_out_ffn_kernel, row0=row0, tiles_per_mod=per_seq // tm, tiles_per_seq=seq_len // tm, tm=tm)

    def trio(n):
        return [pl.BlockSpec((tm, n), lambda i: (i, 0)),
                pl.BlockSpec((FFN_HALO, n), lambda i: (jnp.maximum(i * hb - 1, 0), 0)),
                pl.BlockSpec((FFN_HALO, n), lambda i: (jnp.minimum((i + 1) * hb, last), 0))]

    const = lambda shape: _const_spec(shape, lambda i: (0,) * len(shape))
    return pl.pallas_call(
        kern,
        out_shape=jax.ShapeDtypeStruct((t, D_MODEL), F32),
        grid=(t // tm,),
        in_specs=(trio(D_MODEL) + trio(CONV_CH) + trio(gw) + trio(gw) + trio(gw) + trio(NA_HEADS * NA_DIM)
                  + [const((D_MODEL, D_MODEL)), const((D_MODEL, D_FF)), const((D_MODEL, D_FF)), const((D_FF, D_MODEL)),
                     const((3, D_FF)), const((3, D_FF)), const((1, D_FF)), const((1, D_FF)),
                     _const_spec((1, 6, mod.shape[2], D_MODEL), lambda i: (layer, 0, 0, 0)),
                     const((1, D_MODEL)), const((1, D_MODEL)), const((1, GDN_D)), const((1, D_MODEL))]),
        out_specs=pl.BlockSpec((tm, D_MODEL), lambda i: (i, 0)),
        scratch_shapes=[pltpu.VMEM((tm + 2 * FFN_HALO, D_MODEL), BF16), pltpu.VMEM((tm + 2 * FFN_HALO, D_MODEL), BF16),
                        pltpu.VMEM((tm, D_FF), BF16)],
        compiler_params=_cparams(("parallel",), VMEM_LIMIT),
        name="out_ffn",
    )(x, x, x, oa, oa, oa, of, of, of, ob, ob, ob, rest, rest, rest, oc, oc, oc, w_out, wg, wv, wd, cwg, cwv, cbg, cbv,
      mod, g_post_mix, g_pre_ffn, gnw, g_post_ffn)


def _trunk_layer(x, lw, layer, mod, *, batch, seq, row0, per_seq, ctx):
    tm = min(512, seq)
    rest, oa, qn, kn, vn, bg, bgt = mix_in(
        x, mod, layer, lw["g_pre_mix"], lw["w_in"], lw["conv_w"], lw["conv_b"], lw["conv_ln_g"], lw["conv_ln_b"],
        lw["gdn_conv_w"], lw["alog_row"], lw["dtb_row"], row0=row0, per_seq=per_seq, seq_len=seq, tm=tm)
    gw = GDN_HEADS * GDN_D
    p3 = rest.reshape(batch, seq, REST_COLS)
    qn, kn, vn = (a.reshape(batch, seq, gw) for a in (qn, kn, vn))
    bg = bg.reshape(batch, seq, 128)
    if ctx is None:
        state0 = jnp.zeros((batch, 1, 2, GDN_HEADS, GDN_D, GDN_D), F32)
        of, ob, s_fin = gdn_scan(qn, kn, vn, bg, bgt, state0, 0)
        oc = context_attention(p3)
    else:
        k_ctx, v_ctx, bias_tab, state_delta = ctx
        of, ob, s_fin = gdn_scan(qn, kn, vn, bg, bgt, state_delta, layer)
        oc = neighbourhood_attention(p3, k_ctx, v_ctx, bias_tab)
    t = batch * seq
    x2 = out_ffn(x, oa, of.reshape(t, gw), ob.reshape(t, gw), rest, oc.reshape(t, NA_HEADS * NA_DIM), lw["w_out"],
                 lw["wg"], lw["wv"], lw["wd"], lw["cwg"], lw["cwv"], lw["cbg"], lw["cbv"], mod, layer,
                 lw["g_post_mix"], lw["g_pre_ffn"], lw["gdn_norm_w"], lw["g_post_ffn"],
                 row0=row0, per_seq=per_seq, seq_len=seq, tm=tm)
    return x2, p3, s_fin


def kernel(x_prompt, x_sample, cache_attn_kv, state_delta, c, c_ctx, w_ada, b_ada, g_pre_mix, g_post_mix, g_pre_ffn, g_post_ffn, w_in, w_out, conv_w, conv_b, conv_ln_g, conv_ln_b, gdn_conv_w, gdn_a_log, gdn_dt_bias, gdn_norm_w, na_rpb, w_up, ffn_conv_w, ffn_conv_b, w_down):
    depth = w_in.shape[0]
    bp, sp, d = x_prompt.shape
    bs, ss, _ = x_sample.shape
    rows = ss // GRID_W
    nmod = 16
    assert 1 + bs <= nmod

    n_gdn = 2 * GDN_HEADS
    c0 = 2 * CONV_CH + 4 * GDN_HEADS * GDN_D
    w_in_b = w_in.astype(BF16)
    w_in_r = jnp.concatenate(
        [w_in_b[:, :, :c0], w_in_b[:, :, c0 + 2 * n_gdn:], w_in_b[:, :, c0:c0 + 2 * n_gdn],
         jnp.zeros((depth, d, P_DIM - w_in.shape[2]), BF16)], axis=-1)
    w_out_b = w_out.astype(BF16)
    wg_b = w_up[:, :, :D_FF].astype(BF16)
    wv_b = w_up[:, :, D_FF:].astype(BF16)
    wd_b = w_down.astype(BF16)

    def lane_row(vals):
        return jnp.pad(vals.reshape(depth, 1, n_gdn), ((0, 0), (0, 0), (8, 128 - 8 - n_gdn)))

    alog_rows = lane_row(gdn_a_log)
    dtb_rows = lane_row(gdn_dt_bias)

    cvecs = jnp.concatenate([c_ctx[None, :], c, jnp.zeros((nmod - 1 - bs, d), F32)], axis=0)
    mod = ada_modulation(cvecs, w_ada, b_ada)
    bias_tabs = attention_bias_tables(na_rpb, rows)

    xp = x_prompt.reshape(bp * sp, d)
    xs = x_sample.reshape(bs * ss, d)
    kv_new, st_new = [], []
    hw = NA_HEADS * NA_DIM
    for l in range(depth):
        lw = {
            "g_pre_mix": g_pre_mix[l][None], "g_post_mix": g_post_mix[l][None],
            "g_pre_ffn": g_pre_ffn[l][None], "g_post_ffn": g_post_ffn[l][None],
            "w_in": w_in_r[l], "w_out": w_out_b[l],
            "conv_w": conv_w[l], "conv_b": conv_b[l][None], "conv_ln_g": conv_ln_g[l][None],
            "conv_ln_b": conv_ln_b[l][None],
            "gdn_conv_w": gdn_conv_w[l], "alog_row": alog_rows[l], "dtb_row": dtb_rows[l],
            "gdn_norm_w": gdn_norm_w[l][None],
            "wg": wg_b[l], "wv": wv_b[l], "wd": wd_b[l],
            "cwg": ffn_conv_w[l][:, :D_FF], "cwv": ffn_conv_w[l][:, D_FF:],
            "cbg": ffn_conv_b[l][None, :D_FF], "cbv": ffn_conv_b[l][None, D_FF:],
        }
        xp, p3, s_fin = _trunk_layer(xp, lw, l, mod, batch=bp, seq=sp, row0=0, per_seq=bp * sp, ctx=None)
        heads = lambda col: p3[:, :, col:col + hw].reshape(bp, sp, NA_HEADS, NA_DIM).transpose(0, 2, 1, 3)
        kv_new.append(jnp.stack([heads(REST_KC), heads(REST_VC)], axis=1))
        st_new.append(s_fin)
        lanes = lambda t: t.transpose(0, 2, 1, 3).reshape(bs, t.shape[2], hw)
        ctx = (lanes(cache_attn_kv[:, l, 0]), lanes(cache_attn_kv[:, l, 1]), bias_tabs[l], state_delta)
        xs, _, _ = _trunk_layer(xs, lw, l, mod, batch=bs, seq=ss, row0=1, per_seq=ss, ctx=ctx)
    return (xp.reshape(bp, sp, d), xs.reshape(bs, ss, d), jnp.stack(kv_new, axis=1), jnp.stack(st_new, axis=1))
```

```python
import functools
import math

import jax
import jax.numpy as jnp
from jax import lax
from jax.experimental import pallas as pl
from jax.experimental.pallas import tpu as pltpu

F32 = jnp.float32
BF16 = jnp.bfloat16
EPS = 1e-6
NEG = -1e30

D_MODEL = 1024
CONV_CH = 256
CONV_K = 31
GDN_HEADS = 4
GDN_D = 128
NA_HEADS = 4
NA_DIM = 64
GRID_W = 64
WIN_H = 8
WIN_W = 16
D_FF = 2816
CHUNK = 64
GDN_BLOCK = 1024
GDN_SUB = 256
NA_QROWS = 4
NA_WROWS = 12

COL_AV, COL_AG = 0, 256
COL_Q, COL_K, COL_V, COL_Z = 512, 1024, 1536, 2048
COL_QC, COL_KC, COL_VC = 2560, 2816, 3072
COL_BD = 3328
P_DIM = 3456

VMEM_LIMIT = 56 * 1024 * 1024


def _cparams(sem, vmem=None):
    return pltpu.CompilerParams(dimension_semantics=sem, vmem_limit_bytes=vmem)


def _const_spec(shape, index_map):
    return pl.BlockSpec(shape, index_map, pipeline_mode=pl.Buffered(1))


def _silu(x):
    return x * jax.nn.sigmoid(x)


def _rms(x, g):
    return x * lax.rsqrt(jnp.mean(x * x, axis=-1, keepdims=True) + EPS) * g


def _dot(a, b):
    return jnp.dot(a, b, preferred_element_type=F32)


def _dot_nt(a, b):
    return lax.dot_general(a, b, (((1,), (1,)), ((), ())), preferred_element_type=F32)


def _dot_tn(a, b):
    return lax.dot_general(a, b, (((0,), (0,)), ((), ())), preferred_element_type=F32)


def _mm(a, b):
    return _dot(a.astype(BF16), b.astype(BF16))


def _half_rows(x, s, half):
    n, c = x.shape[0] // (2 * s), x.shape[1]
    return x.reshape(n, 2, s, c)[:, half].reshape(n * s, c)


def _expand_rows(y, s, half):
    n, c = y.shape[0] // s, y.shape[1]
    y4 = y.reshape(n, 1, s, c)
    z = jnp.zeros_like(y4)
    return jnp.concatenate([z, y4] if half == 1 else [y4, z], axis=1).reshape(2 * n * s, c)


def _ada_kernel(c_ref, w_ref, b_ref, o_ref):
    s = _silu(c_ref[...]).astype(BF16)
    o_ref[0, 0] = _dot(s, w_ref[0].astype(BF16)) + b_ref[0, 0]


def ada_modulation(cvecs, w_ada, b_ada):
    depth = w_ada.shape[0]
    n = cvecs.shape[0]
    b4 = b_ada.reshape(depth, 6, 1, D_MODEL)
    return pl.pallas_call(
        _ada_kernel,
        out_shape=jax.ShapeDtypeStruct((depth, 6, n, D_MODEL), F32),
        grid=(depth, 6),
        in_specs=[pl.BlockSpec((n, D_MODEL), lambda l, j: (0, 0)),
                  pl.BlockSpec((1, D_MODEL, D_MODEL), lambda l, j: (l, 0, j)),
                  pl.BlockSpec((1, 1, 1, D_MODEL), lambda l, j: (l, j, 0, 0))],
        out_specs=pl.BlockSpec((1, 1, n, D_MODEL), lambda l, j: (l, j, 0, 0)),
        compiler_params=_cparams(("arbitrary", "arbitrary")),
        name="ada_modulation",
    )(cvecs, w_ada, b4)


MIX_HALO = 16
CONF_ROWS = 64
BG_ROWS = 24
REST_COLS = COL_BD - COL_Z
REST_Z, REST_QC, REST_KC, REST_VC = 0, COL_QC - COL_Z, COL_KC - COL_Z, COL_VC - COL_Z


def _mix_in_kernel(x_ref, xp_ref, xn_ref, mod_ref, g_ref, wh_ref, wt_ref, cw_ref, cb_ref, lg_ref, lb_ref, gcw_ref,
                   alog_ref, dtb_ref, rest_ref, oa_ref, qo_ref, ko_ref, vo_ref, bg_ref, bgt_ref,
                   hext_scr, u_scr, rot_scr, *, row0, tiles_per_mod, tiles_per_seq, tm):
    i = pl.program_id(0)
    row = row0 + i // tiles_per_mod
    t_in_seq = i % tiles_per_seq
    keep_prev = (t_in_seq > 0).astype(F32)
    keep_next = (t_in_seq < tiles_per_seq - 1).astype(F32)
    sh = mod_ref[0, 0, pl.ds(row, 1), :]
    sc = mod_ref[0, 1, pl.ds(row, 1), :]
    hh = MIX_HALO
    gw = GDN_HEADS * GDN_D

    def norm(x):
        return (_rms(x, g_ref[...]) * (1.0 + sc) + sh).astype(BF16)

    hext_scr[0:hh, :] = norm(xp_ref[...])
    hext_scr[hh:hh + tm, :] = norm(x_ref[...])
    hext_scr[hh + tm:2 * hh + tm, :] = norm(xn_ref[...])

    def proj_ext(col, width):
        return _dot(hext_scr[...], wh_ref[:, col:col + width])

    def proj(col, width):
        w_ref, col = (wh_ref, col) if col < COL_QC else (wt_ref, col - COL_QC)
        return _dot(hext_scr[hh:hh + tm, :], w_ref[:, col:col + width])

    pa = proj_ext(COL_AV, 2 * CONV_CH)
    pq = proj_ext(COL_Q, gw)
    u = pa[:, :CONV_CH] * jax.nn.sigmoid(pa[:, CONV_CH:])
    u_scr[0:hh, :] = u[0:hh] * keep_prev
    u_scr[hh:hh + tm, :] = u[hh:hh + tm]
    u_scr[hh + tm:2 * hh + tm, :] = u[hh + tm:] * keep_next
    n_ext = tm + 2 * hh
    u_ext = u_scr[...]
    for r in range(1, 8):
        rot_scr[r - 1] = pltpu.roll(u_ext, n_ext - r, 0)
    pk = proj_ext(COL_K, gw)
    half = CONV_K // 2
    for r0 in range(0, tm, CONF_ROWS):
        acc = jnp.zeros((CONF_ROWS, CONV_CH), F32) + cb_ref[...]
        for k in range(CONV_K):
            off = r0 + k + hh - half
            r = off % 8
            src = u_scr[off:off + CONF_ROWS, :] if r == 0 else rot_scr[r - 1, off - r:off - r + CONF_ROWS, :]
            acc = acc + cw_ref[pl.ds(k, 1), :] * src
        mu = jnp.mean(acc, axis=-1, keepdims=True)
        cen = acc - mu
        var = jnp.mean(cen * cen, axis=-1, keepdims=True)
        y = cen * lax.rsqrt(var + EPS) * lg_ref[...] + lb_ref[...]
        oa_ref[r0:r0 + CONF_ROWS, :] = _silu(y).astype(BF16)

    row8 = lax.broadcasted_iota(jnp.int32, (8, gw), 0)

    def conv_silu(pe, j):
        x = pe[hh:hh + tm]
        prev_row = pe[hh - 1:hh] * keep_prev
        next_row = pe[hh + tm:hh + tm + 1] * keep_next
        up = pltpu.roll(x, 1, 0)
        dn = pltpu.roll(x, tm - 1, 0)
        up = jnp.concatenate([jnp.where(row8 == 0, prev_row, up[0:8]), up[8:]], axis=0)
        dn = jnp.concatenate([dn[:tm - 8], jnp.where(row8 == 7, next_row, dn[tm - 8:])], axis=0)
        w = gcw_ref[:, j * gw:(j + 1) * gw]
        return _silu(up * w[0:1, :] + x * w[1:2, :] + dn * w[2:3, :])

    def l2n(seg, scale):
        return seg * (lax.rsqrt(jnp.sum(seg * seg, axis=-1, keepdims=True) + EPS) * scale)

    pv = proj_ext(COL_V, gw)
    q = conv_silu(pq, 0)
    for h in range(GDN_HEADS):
        sl = slice(h * GDN_D, (h + 1) * GDN_D)
        qo_ref[:, sl] = l2n(q[:, sl], GDN_D ** -0.5)
    rest_ref[:, 0:gw] = proj(COL_Z, gw)
    k = conv_silu(pk, 1)
    for h in range(GDN_HEADS):
        sl = slice(h * GDN_D, (h + 1) * GDN_D)
        ko_ref[:, sl] = l2n(k[:, sl], 1.0)
    rest_ref[:, gw:2 * gw] = proj(COL_QC, gw)
    vo_ref[...] = conv_silu(pv, 2)
    tail = proj(COL_VC, P_DIM - COL_VC)
    rest_ref[:, 2 * gw:REST_COLS] = tail[:, :REST_COLS - 2 * gw]

    bd = tail[:, REST_COLS - 2 * gw:]
    lane = lax.broadcasted_iota(jnp.int32, (tm, 128), 1)
    rc = lax.broadcasted_iota(jnp.int32, (tm, 128), 0) & (CHUNK - 1)
    is_g = (lane >= 8) & (lane < 16)
    beta = jax.nn.sigmoid(bd)
    xg = bd + dtb_ref[...]
    softplus = jnp.maximum(xg, 0.0) + jnp.log1p(jnp.exp(-jnp.abs(xg)))
    g = jnp.where(is_g, -jnp.exp(alog_ref[...]) * softplus, 0.0)
    pre = g
    suf = g
    s = 1
    while s < CHUNK:
        pre = pre + jnp.where(rc >= s, pltpu.roll(pre, s, 0), 0.0)
        suf = suf + jnp.where(rc < CHUNK - s, pltpu.roll(suf, tm - s, 0), 0.0)
        s *= 2
    tot = pre + suf - g
    bg = jnp.where(lane < 8, beta,
                   jnp.where(lane < 12, pre,
                             jnp.where(lane < 16, suf, pltpu.roll(tot, 8, 1))))
    bg = jnp.where(lane < 24, bg, 0.0)
    bg_ref[...] = bg
    bgt_ref[0] = jnp.transpose(bg)[0:BG_ROWS, :]


def mix_in(x, mod, layer, g_pre, w_head, w_tail, conv_w, conv_b, ln_g, ln_b, gdn_conv_w, alog_row, dtb_row, *,
           row0, per_seq, seq_len, tm):
    t = x.shape[0]
    assert seq_len % tm == 0 and per_seq % seq_len == 0 and tm % CONF_ROWS == 0 and tm % CHUNK == 0
    gw = GDN_HEADS * GDN_D
    hb = tm // MIX_HALO
    last = t // MIX_HALO - 1
    tps = seq_len // tm
    kern = functools.partial(_mix_in_kernel, row0=row0, tiles_per_mod=per_seq // tm, tiles_per_seq=tps, tm=tm)
    rowblk = lambda n: pl.BlockSpec((tm, n), lambda i: (i, 0))
    const = lambda shape: _const_spec(shape, lambda i: (0,) * len(shape))
    seq = jax.ShapeDtypeStruct((t, gw), F32)
    return pl.pallas_call(
        kern,
        out_shape=(jax.ShapeDtypeStruct((t, REST_COLS), F32), jax.ShapeDtypeStruct((t, CONV_CH), BF16),
                   seq, seq, seq, jax.ShapeDtypeStruct((t, 128), F32),
                   jax.ShapeDtypeStruct((t // seq_len, BG_ROWS, seq_len), F32)),
        grid=(t // tm,),
        in_specs=[rowblk(D_MODEL),
                  pl.BlockSpec((MIX_HALO, D_MODEL), lambda i: (jnp.maximum(i * hb - 1, 0), 0)),
                  pl.BlockSpec((MIX_HALO, D_MODEL), lambda i: (jnp.minimum((i + 1) * hb, last), 0)),
                  _const_spec((1, 6, mod.shape[2], D_MODEL), lambda i: (layer, 0, 0, 0)),
                  const((1, D_MODEL)), const((D_MODEL, COL_QC)), const((D_MODEL, P_DIM - COL_QC)),
                  const((CONV_K, CONV_CH)), const((1, CONV_CH)), const((1, CONV_CH)), const((1, CONV_CH)),
                  const((3, 3 * gw)), const((1, 128)), const((1, 128))],
        out_specs=(rowblk(REST_COLS), rowblk(CONV_CH), rowblk(gw), rowblk(gw), rowblk(gw), rowblk(128),
                   pl.BlockSpec((1, BG_ROWS, tm), lambda i: (i // tps, 0, i % tps))),
        scratch_shapes=[pltpu.VMEM((tm + 2 * MIX_HALO, D_MODEL), BF16),
                        pltpu.VMEM((tm + 2 * MIX_HALO, CONV_CH), F32),
                        pltpu.VMEM((7, tm + 2 * MIX_HALO, CONV_CH), F32)],
        compiler_params=_cparams(("parallel",), VMEM_LIMIT),
        name="mix_in",
    )(x, x, x, mod, g_pre, w_head, w_tail, conv_w, conv_b, ln_g, ln_b, gdn_conv_w, alog_row, dtb_row)


def _gscan_kernel(qf_ref, kf_ref, vf_ref, bgf_ref, bgtf_ref, qb_ref, kb_ref, vb_ref, bgb_ref, bgtb_ref,
                  s0_ref, of_ref, ob_ref, sfin_ref, s_scr, vn_scr, *, r):
    nb = pl.program_id(1)
    sub = min(GDN_SUB, r)
    nsub = r // sub
    nchunk = sub // CHUNK
    shift = int(math.log2(CHUNK))

    @pl.when(nb == 0)
    def _():
        s_scr[...] = s0_ref[0, 0]

    vn_scr[...] = jnp.zeros((2 * GDN_HEADS, sub, GDN_D), F32)
    ii = lax.broadcasted_iota(jnp.int32, (sub, sub), 0)
    jj = lax.broadcasted_iota(jnp.int32, (sub, sub), 1)
    blk_same = {s: (ii >> s) == (jj >> s) for s in range(3, shift + 1)}
    same = blk_same[shift]
    eye = (ii == jj).astype(F32)
    dirs = ((qf_ref, kf_ref, vf_ref, bgf_ref, bgtf_ref, of_ref), (qb_ref, kb_ref, vb_ref, bgb_ref, bgtb_ref, ob_ref))
    masks = ((same & (ii >= jj), same & (ii > jj)), (same & (ii <= jj), same & (ii < jj)))
    sl_h = [slice(h * GDN_D, (h + 1) * GDN_D) for h in range(GDN_HEADS)]
    nblk = 2 * GDN_HEADS

    def prepare(group, out):
        rs = [slice(sb * sub, (sb + 1) * sub) for sb, d, h in group]

        def each(fn):
            return [fn(i, group[i][1], group[i][2]) for i in range(nblk)]

        def col(off):
            return each(lambda i, d, h: dirs[d][3][0, rs[i], off + d * GDN_HEADS + h:off + d * GDN_HEADS + h + 1])

        beta, gcc, gl = col(0), col(8), col(16)
        gcr = each(lambda i, d, h: dirs[d][4][0, 8 + d * GDN_HEADS + h:9 + d * GDN_HEADS + h, rs[i]])
        q = each(lambda i, d, h: dirs[d][0][0, rs[i], sl_h[h]])
        k = each(lambda i, d, h: dirs[d][1][0, rs[i], sl_h[h]])
        v = each(lambda i, d, h: dirs[d][2][0, rs[i], sl_h[h]])
        decay = each(lambda i, d, h: jnp.exp(jnp.where(masks[d][0], gcc[i] - gcr[i], NEG)))
        kbeta = each(lambda i, d, h: k[i] * beta[i])
        kb16 = each(lambda i, d, h: k[i].astype(BF16))
        yield
        lm = each(lambda i, d, h: jnp.where(masks[d][1], _dot_nt(kbeta[i].astype(BF16), kb16[i]) * decay[i], 0.0))
        egc = each(lambda i, d, h: jnp.exp(gcc[i]))
        x = each(lambda i, d, h: jnp.concatenate([v[i] * beta[i], kbeta[i] * egc[i]], axis=1))
        yield
        l8 = each(lambda i, d, h: jnp.where(blk_same[3], lm[i], 0.0))
        m2 = each(lambda i, d, h: _mm(l8[i], l8[i]))
        yield
        m4 = each(lambda i, d, h: _mm(m2[i], m2[i]))
        t = each(lambda i, d, h: eye - l8[i])
        t = each(lambda i, d, h: t[i] + _mm(t[i], m2[i]))
        yield
        t = each(lambda i, d, h: t[i] + _mm(t[i], m4[i]))
        yield
        for lvl in range(3, shift):
            s_rows = 1 << lvl
            off = blk_same[lvl + 1] & jnp.logical_not(blk_same[lvl])
            ct = each(lambda i, d, h: _mm(_half_rows(jnp.where(off, lm[i], 0.0), s_rows, 1 - d), t[i]))
            yield
            upd = each(lambda i, d, h: _mm(_half_rows(t[i], s_rows, 1 - d), _expand_rows(ct[i], s_rows, 1 - d)))
            t = each(lambda i, d, h: t[i] - _expand_rows(upd[i], s_rows, 1 - d))
            yield
        out["x"] = each(lambda i, d, h: _mm(t[i], x[i]))
        yield
        out["intra"] = each(lambda i, d, h: (_dot_nt(q[i].astype(BF16), kb16[i]) * decay[i]).astype(BF16))
        out["qd"] = each(lambda i, d, h: q[i] * egc[i])
        out["kd"] = each(lambda i, d, h: (k[i] * jnp.exp(gl[i] - gcc[i])).astype(BF16))
        out["eg_last"] = each(lambda i, d, h: jnp.exp(jnp.broadcast_to(gl[i], (sub, GDN_D))))
        yield

    def scan(group, pre, state):
        x, qd, kd, intra, eg_last = pre["x"], pre["qd"], pre["kd"], pre["intra"], pre["eg_last"]
        outs = [[None] * nchunk for _ in range(nblk)]
        for step in range(nchunk):
            cs = [step if d == 0 else nchunk - 1 - step for sb, d, h in group]
            sls = [slice(c * CHUNK, (c + 1) * CHUNK) for c in cs]
            a = [_dot(jnp.concatenate([x[i][sls[i], GDN_D:], qd[i][sls[i]]], axis=0).astype(BF16),
                      state[group[i][1:]].astype(BF16)) for i in range(nblk)]
            v_new = [x[i][sls[i], :GDN_D] - a[i][:CHUNK] for i in range(nblk)]
            for i in range(nblk):
                vn_scr[i, sls[i], :] = v_new[i]
            yield
            for i in range(nblk):
                outs[i][cs[i]] = a[i][CHUNK:] + _dot(intra[i][sls[i], :], vn_scr[i].astype(BF16))
            for i in range(nblk):
                key = group[i][1:]
                state[key] = (state[key] * eg_last[i][cs[i] * CHUNK:cs[i] * CHUNK + 1, :]
                              + _dot_tn(kd[i][sls[i]], v_new[i].astype(BF16)))
            yield
        for i, (sb, d, h) in enumerate(group):
            dirs[d][5][0, sb * sub:(sb + 1) * sub, sl_h[h]] = jnp.concatenate(outs[i], axis=0)

    def run(*gens):
        gens = [g for g in gens if g is not None]
        while gens:
            for g in list(gens):
                try:
                    next(g)
                except StopIteration:
                    gens.remove(g)

    groups = [[(p, 0, h) for h in range(GDN_HEADS)] + [(nsub - 1 - p, 1, h) for h in range(GDN_HEADS)]
              for p in range(nsub)]
    state = {(d, h): s_scr[d, h] for d in range(2) for h in range(GDN_HEADS)}
    prev = None
    for p in range(nsub):
        pre = {}
        run(prepare(groups[p], pre), scan(groups[p - 1], prev, state) if p else None)
        prev = pre
    run(scan(groups[nsub - 1], prev, state))
    for (d, h), val in state.items():
        s_scr[d, h] = val
    sfin_ref[0] = s_scr[...]


def gdn_scan(q, k, v, bg, bgt, state0, layer):
    b, l, gw = q.shape
    r = min(GDN_BLOCK, l)
    nbk = l // r
    kern = functools.partial(_gscan_kernel, r=r)
    fwd = lambda i, n: (i, n, 0)
    bwd = lambda i, n: (i, nbk - 1 - n, 0)
    fwd_t = lambda i, n: (i, 0, n)
    bwd_t = lambda i, n: (i, 0, nbk - 1 - n)
    seq = lambda im: pl.BlockSpec((1, r, gw), im)
    st_shape = (2, GDN_HEADS, GDN_D, GDN_D)
    return pl.pallas_call(
        kern,
        out_shape=(jax.ShapeDtypeStruct((b, l, gw), F32), jax.ShapeDtypeStruct((b, l, gw), F32),
                   jax.ShapeDtypeStruct((b,) + st_shape, F32)),
        grid=(b, nbk),
        in_specs=[seq(fwd), seq(fwd), seq(fwd), pl.BlockSpec((1, r, 128), fwd), pl.BlockSpec((1, BG_ROWS, r), fwd_t),
                  seq(bwd), seq(bwd), seq(bwd), pl.BlockSpec((1, r, 128), bwd), pl.BlockSpec((1, BG_ROWS, r), bwd_t),
                  pl.BlockSpec((1, 1) + st_shape, lambda i, n: (i, layer, 0, 0, 0, 0))],
        out_specs=(seq(fwd), seq(bwd), pl.BlockSpec((1,) + st_shape, lambda i, n: (i, 0, 0, 0, 0))),
        scratch_shapes=[pltpu.VMEM(st_shape, F32), pltpu.VMEM((2 * GDN_HEADS, min(GDN_SUB, r), GDN_D), F32)],
        compiler_params=_cparams(("parallel", "arbitrary"), VMEM_LIMIT),
        name="gdn_scan",
    )(q, k, v, bg, bgt, q, k, v, bg, bgt, state0)


def _softmax_pv(q, parts):
    heads = range(NA_HEADS)
    lane = lax.broadcasted_iota(jnp.int32, (1, NA_HEADS * NA_DIM), 1)
    hm = [(lane >= h * NA_DIM) & (lane < (h + 1) * NA_DIM) for h in heads]
    qh = [jnp.where(hm[h], q, 0.0).astype(BF16) for h in heads]
    s = [[_dot_nt(qh[h], keys) if bias is None else _dot_nt(qh[h], keys) + bias(h) for keys, _, bias in parts]
         for h in heads]
    mx = [functools.reduce(jnp.maximum, [jnp.max(sp, axis=-1, keepdims=True) for sp in s[h]]) for h in heads]
    e = [[jnp.exp(sp - mx[h]) for sp in s[h]] for h in heads]
    inv = [1.0 / sum(jnp.sum(ep, axis=-1, keepdims=True) for ep in e[h]) for h in heads]
    o = [sum(_dot(ep.astype(BF16), vals) for ep, (_, vals, _) in zip(e[h], parts)) * inv[h] for h in heads]
    out = o[NA_HEADS - 1]
    for h in range(NA_HEADS - 2, -1, -1):
        out = jnp.where(hm[h], o[h], out)
    return out


def _attn_ctx_kernel(q_ref, k_ref, v_ref, o_ref):
    q = q_ref[0] * (NA_DIM ** -0.5)
    o_ref[0] = _softmax_pv(q, [(k_ref[0].astype(BF16), v_ref[0].astype(BF16), None)]).astype(BF16)


def context_attention(p3):
    b, l, _ = p3.shape
    w = NA_HEADS * NA_DIM
    col = lambda c: pl.BlockSpec((1, l, w), lambda i: (i, 0, c // w))
    return pl.pallas_call(
        _attn_ctx_kernel,
        out_shape=jax.ShapeDtypeStruct((b, l, w), BF16),
        grid=(b,),
        in_specs=[col(REST_QC), col(REST_KC), col(REST_VC)],
        out_specs=pl.BlockSpec((1, l, w), lambda i: (i, 0, 0)),
        compiler_params=_cparams(("parallel",)),
        name="context_attention",
    )(p3, p3, p3)


def _attn_lat_kernel(q_ref, k_ref, v_ref, kc_ref, vc_ref, bias_ref, o_ref, *, rows):
    i = pl.program_id(1)
    nwin = NA_WROWS * GRID_W
    ws = jnp.clip(i * NA_QROWS - WIN_H // 2, 0, rows - NA_WROWS)
    start = pl.multiple_of(ws * GRID_W, GRID_W)
    q = q_ref[0] * (NA_DIM ** -0.5)
    window = (k_ref[0, pl.ds(start, nwin), :].astype(BF16), v_ref[0, pl.ds(start, nwin), :].astype(BF16),
              lambda h: bias_ref[0, 0, h])
    context = (kc_ref[0, 0, 0].astype(BF16), vc_ref[0, 0, 0].astype(BF16), None)
    o_ref[0] = _softmax_pv(q, [window, context]).astype(BF16)


def neighbourhood_attention(p3, kv_ctx, bias_tabs, layer):
    b, n, _ = p3.shape
    rows = n // GRID_W
    w = NA_HEADS * NA_DIM
    tq = NA_QROWS * GRID_W
    ni = rows // NA_QROWS
    nctx = kv_ctx.shape[3]
    kern = functools.partial(_attn_lat_kernel, rows=rows)
    case = lambda i, t: (layer, jnp.where(t == 0, 0, jnp.where(t == ni - 1, 2, 1)), 0, 0, 0)
    return pl.pallas_call(
        kern,
        out_shape=jax.ShapeDtypeStruct((b, n, w), BF16),
        grid=(b, ni),
        in_specs=[pl.BlockSpec((1, tq, w), lambda i, t: (i, t, REST_QC // w)),
                  pl.BlockSpec((1, n, w), lambda i, t: (i, 0, REST_KC // w)),
                  pl.BlockSpec((1, n, w), lambda i, t: (i, 0, REST_VC // w)),
                  pl.BlockSpec((1, 1, 1, nctx, w), lambda i, t: (i, layer, 0, 0, 0)),
                  pl.BlockSpec((1, 1, 1, nctx, w), lambda i, t: (i, layer, 1, 0, 0)),
                  pl.BlockSpec((1, 1, NA_HEADS, tq, NA_WROWS * GRID_W), case)],
        out_specs=pl.BlockSpec((1, tq, w), lambda i, t: (i, t, 0)),
        compiler_params=_cparams(("parallel", "arbitrary"), VMEM_LIMIT),
        name="neighbourhood_attention",
    )(p3, p3, p3, kv_ctx, kv_ctx, bias_tabs)


def _bias_kernel(rpb_ref, o_ref, *, rows):
    l = pl.program_id(0)
    h = pl.program_id(1)
    n_dr, n_dc = 2 * WIN_H - 1, 2 * WIN_W - 1
    base = (l * NA_HEADS + h) * (n_dr * n_dc)
    c = lax.broadcasted_iota(jnp.int32, (GRID_W, 2 * GRID_W), 0)
    lane = lax.broadcasted_iota(jnp.int32, (GRID_W, 2 * GRID_W), 1)
    kc = lane & (GRID_W - 1)
    dc = jnp.clip(kc - c + (WIN_W - 1), 0, n_dc - 1)
    cs = jnp.clip(c - WIN_W // 2, 0, GRID_W - WIN_W)
    col_ok = (kc >= cs) & (kc < cs + WIN_W)
    neg = jnp.full((GRID_W, 2 * GRID_W), NEG, F32)
    tabs = []
    for a in range(n_dr):
        t = jnp.zeros((GRID_W, 2 * GRID_W), F32)
        for bb in range(n_dc):
            t = jnp.where(dc == bb, rpb_ref[base + a * n_dc + bb], t)
        tabs.append(jnp.where(col_ok, t, NEG))
    interior = 2 * NA_QROWS
    cases = ((0, 0), (interior, interior - WIN_H // 2), (rows - NA_QROWS, rows - NA_WROWS))
    for case, (r0, ws) in enumerate(cases):
        for iq in range(NA_QROWS):
            r = r0 + iq
            rs = min(max(r - WIN_H // 2, 0), rows - WIN_H)
            for jp in range(NA_WROWS // 2):
                halves = []
                for half in range(2):
                    kr = ws + 2 * jp + half
                    halves.append(tabs[kr - r + WIN_H - 1] if rs <= kr < rs + WIN_H else neg)
                blk = jnp.where(lane < GRID_W, halves[0], halves[1])
                o_ref[0, case, 0, iq * GRID_W:(iq + 1) * GRID_W, jp * 2 * GRID_W:(jp + 1) * 2 * GRID_W] = blk


def attention_bias_tables(na_rpb, rows):
    depth = na_rpb.shape[0]
    kern = functools.partial(_bias_kernel, rows=rows)
    tq, nw = NA_QROWS * GRID_W, NA_WROWS * GRID_W
    return pl.pallas_call(
        kern,
        out_shape=jax.ShapeDtypeStruct((depth, 3, NA_HEADS, tq, nw), F32),
        grid=(depth, NA_HEADS),
        in_specs=[pl.BlockSpec(memory_space=pltpu.SMEM)],
        out_specs=pl.BlockSpec((1, 3, 1, tq, nw), lambda l, h: (l, 0, h, 0, 0)),
        compiler_params=_cparams(("arbitrary", "arbitrary")),
        name="attention_bias_tables",
    )(na_rpb.reshape(-1))


FFN_HALO = 16
FFN_CHUNK = 256


def _out_ffn_kernel(x_ref, xp_ref, xn_ref, oa_ref, oap_ref, oan_ref, of_ref, ofp_ref, ofn_ref,
                    ob_ref, obp_ref, obn_ref, z_ref, zp_ref, zn_ref, oc_ref, ocp_ref, ocn_ref,
                    wo_ref, wg_ref, wv_ref, wd_ref, cwg_ref, cwv_ref, cbg_ref, cbv_ref,
                    mod_ref, gpm_ref, gpf_ref, gnw_ref, gpo_ref, o_ref,
                    mix_scr, hext_scr, act_scr, *, row0, tiles_per_mod, tiles_per_seq, tm):
    i = pl.program_id(0)
    row = row0 + i // tiles_per_mod
    g1 = mod_ref[0, 2, pl.ds(row, 1), :]
    sh2 = mod_ref[0, 3, pl.ds(row, 1), :]
    sc2 = mod_ref[0, 4, pl.ds(row, 1), :]
    g2 = mod_ref[0, 5, pl.ds(row, 1), :]
    hh = FFN_HALO
    gw = GDN_HEADS * GDN_D
    parts = ((0, tm, x_ref, oa_ref, of_ref, ob_ref, z_ref, oc_ref),
             (tm, hh, xp_ref, oap_ref, ofp_ref, obp_ref, zp_ref, ocp_ref),
             (tm + hh, hh, xn_ref, oan_ref, ofn_ref, obn_ref, zn_ref, ocn_ref))
    for r0, n, _, a_ref, f_ref, b_ref, zz_ref, c_ref in parts:
        mix_scr[r0:r0 + n, 0:CONV_CH] = a_ref[...]
        og = f_ref[...] + b_ref[...]
        zz = zz_ref[...]
        for h in range(GDN_HEADS):
            sl = slice(h * GDN_D, (h + 1) * GDN_D)
            gated = _rms(og[:, sl], gnw_ref[...]) * _silu(zz[:, sl])
            mix_scr[r0:r0 + n, CONV_CH + h * GDN_D:CONV_CH + (h + 1) * GDN_D] = gated.astype(BF16)
        mix_scr[r0:r0 + n, CONV_CH + gw:] = c_ref[...]
    m = _dot(mix_scr[...], wo_ref[...])
    x1 = None
    for r0, n, xx_ref, *_ in parts:
        x1_part = xx_ref[...] + g1 * _rms(m[r0:r0 + n], gpm_ref[...])
        hext_scr[r0:r0 + n, :] = (_rms(x1_part, gpf_ref[...]) * (1.0 + sc2) + sh2).astype(BF16)
        if r0 == 0:
            x1 = x1_part
    t_in_seq = i % tiles_per_seq
    keep_prev = (t_in_seq > 0).astype(F32)
    keep_next = (t_in_seq < tiles_per_seq - 1).astype(F32)
    row8 = lax.broadcasted_iota(jnp.int32, (8, FFN_CHUNK), 0)

    def conv(u_ext, cw_ref, cb_ref, sl):
        u = u_ext[0:tm]
        prev_row = u_ext[tm + hh - 1:tm + hh] * keep_prev
        next_row = u_ext[tm + hh:tm + hh + 1] * keep_next
        up = pltpu.roll(u, 1, 0)
        dn = pltpu.roll(u, tm - 1, 0)
        up = jnp.concatenate([jnp.where(row8 == 0, prev_row, up[0:8]), up[8:]], axis=0)
        dn = jnp.concatenate([dn[:tm - 8], jnp.where(row8 == 7, next_row, dn[tm - 8:])], axis=0)
        return up * cw_ref[0:1, sl] + u * cw_ref[1:2, sl] + dn * cw_ref[2:3, sl] + cb_ref[:, sl]

    n_chunks = D_FF // FFN_CHUNK
    sls = [slice(c * FFN_CHUNK, (c + 1) * FFN_CHUNK) for c in range(n_chunks)]

    def up_proj(c):
        return _dot(hext_scr[...], wg_ref[:, sls[c]]), _dot(hext_scr[...], wv_ref[:, sls[c]])

    nxt = up_proj(0)
    for c in range(n_chunks):
        ug, uv = nxt
        if c + 1 < n_chunks:
            nxt = up_proj(c + 1)
        gate = conv(ug, cwg_ref, cbg_ref, sls[c])
        val = conv(uv, cwv_ref, cbv_ref, sls[c])
        act_scr[:, sls[c]] = (_silu(gate) * val).astype(BF16)
    f = _dot(act_scr[...], wd_ref[...])
    o_ref[...] = x1 + g2 * _rms(f, gpo_ref[...])


def out_ffn(x, oa, of, ob, rest, oc, w_out, wg, wv, wd, cwg, cwv, cbg, cbv, mod, layer, g_post_mix, g_pre_ffn, gnw,
            g_post_ffn, *, row0, per_seq, seq_len, tm):
    t = x.shape[0]
    assert seq_len % tm == 0 and per_seq % seq_len == 0
    gw = GDN_HEADS * GDN_D
    hb = tm // FFN_HALO
    last = t // FFN_HALO - 1
    kern = functools.partial(_out_ffn_kernel, row0=row0, tiles_per_mod=per_seq // tm, tiles_per_seq=seq_len // tm, tm=tm)

    def trio(n):
        return [pl.BlockSpec((tm, n), lambda i: (i, 0)),
                pl.BlockSpec((FFN_HALO, n), lambda i: (jnp.maximum(i * hb - 1, 0), 0)),
                pl.BlockSpec((FFN_HALO, n), lambda i: (jnp.minimum((i + 1) * hb, last), 0))]

    const = lambda shape: _const_spec(shape, lambda i: (0,) * len(shape))
    return pl.pallas_call(
        kern,
        out_shape=jax.ShapeDtypeStruct((t, D_MODEL), F32),
        grid=(t // tm,),
        in_specs=(trio(D_MODEL) + trio(CONV_CH) + trio(gw) + trio(gw) + trio(gw) + trio(NA_HEADS * NA_DIM)
                  + [const((D_MODEL, D_MODEL)), const((D_MODEL, D_FF)), const((D_MODEL, D_FF)), const((D_FF, D_MODEL)),
                     const((3, D_FF)), const((3, D_FF)), const((1, D_FF)), const((1, D_FF)),
                     _const_spec((1, 6, mod.shape[2], D_MODEL), lambda i: (layer, 0, 0, 0)),
                     const((1, D_MODEL)), const((1, D_MODEL)), const((1, GDN_D)), const((1, D_MODEL))]),
        out_specs=pl.BlockSpec((tm, D_MODEL), lambda i: (i, 0)),
        scratch_shapes=[pltpu.VMEM((tm + 2 * FFN_HALO, D_MODEL), BF16), pltpu.VMEM((tm + 2 * FFN_HALO, D_MODEL), BF16),
                        pltpu.VMEM((tm, D_FF), BF16)],
        compiler_params=_cparams(("parallel",), VMEM_LIMIT),
        name="out_ffn",
    )(x, x, x, oa, oa, oa, of, of, of, ob, ob, ob, rest, rest, rest, oc, oc, oc, w_out, wg, wv, wd, cwg, cwv, cbg, cbv,
      mod, g_post_mix, g_pre_ffn, gnw, g_post_ffn)


def _trunk_layer(x, lw, layer, mod, *, batch, seq, row0, per_seq, ctx):
    tm = min(512, seq)
    rest, oa, qn, kn, vn, bg, bgt = mix_in(
        x, mod, layer, lw["g_pre_mix"], lw["w_head"], lw["w_tail"], lw["conv_w"], lw["conv_b"], lw["conv_ln_g"],
        lw["conv_ln_b"], lw["gdn_conv_w"], lw["alog_row"], lw["dtb_row"],
        row0=row0, per_seq=per_seq, seq_len=seq, tm=tm)
    gw = GDN_HEADS * GDN_D
    p3 = rest.reshape(batch, seq, REST_COLS)
    qn, kn, vn = (a.reshape(batch, seq, gw) for a in (qn, kn, vn))
    bg = bg.reshape(batch, seq, 128)
    if ctx is None:
        state0 = jnp.zeros((batch, 1, 2, GDN_HEADS, GDN_D, GDN_D), F32)
        of, ob, s_fin = gdn_scan(qn, kn, vn, bg, bgt, state0, 0)
        oc = context_attention(p3)
    else:
        kv_ctx, bias_tabs, state_delta = ctx
        of, ob, s_fin = gdn_scan(qn, kn, vn, bg, bgt, state_delta, layer)
        oc = neighbourhood_attention(p3, kv_ctx, bias_tabs, layer)
    t = batch * seq
    x2 = out_ffn(x, oa, of.reshape(t, gw), ob.reshape(t, gw), rest, oc.reshape(t, NA_HEADS * NA_DIM), lw["w_out"],
                 lw["wg"], lw["wv"], lw["wd"], lw["cwg"], lw["cwv"], lw["cbg"], lw["cbv"], mod, layer,
                 lw["g_post_mix"], lw["g_pre_ffn"], lw["gdn_norm_w"], lw["g_post_ffn"],
                 row0=row0, per_seq=per_seq, seq_len=seq, tm=tm)
    return x2, p3, s_fin


def kernel(x_prompt, x_sample, cache_attn_kv, state_delta, c, c_ctx, w_ada, b_ada, g_pre_mix, g_post_mix, g_pre_ffn, g_post_ffn, w_in, w_out, conv_w, conv_b, conv_ln_g, conv_ln_b, gdn_conv_w, gdn_a_log, gdn_dt_bias, gdn_norm_w, na_rpb, w_up, ffn_conv_w, ffn_conv_b, w_down):
    depth = w_in.shape[0]
    bp, sp, d = x_prompt.shape
    bs, ss, _ = x_sample.shape
    rows = ss // GRID_W
    nmod = 16
    assert 1 + bs <= nmod

    n_gdn = 2 * GDN_HEADS
    c0 = 2 * CONV_CH + 4 * GDN_HEADS * GDN_D
    w_head_b = w_in[:, :, :c0].astype(BF16)
    w_tail_b = jnp.concatenate(
        [w_in[:, :, c0 + 2 * n_gdn:], w_in[:, :, c0:c0 + 2 * n_gdn],
         jnp.zeros((depth, d, P_DIM - w_in.shape[2]), w_in.dtype)], axis=-1).astype(BF16)
    w_out_b = w_out.astype(BF16)
    wg_b = w_up[:, :, :D_FF].astype(BF16)
    wv_b = w_up[:, :, D_FF:].astype(BF16)
    wd_b = w_down.astype(BF16)

    def lane_row(vals):
        return jnp.pad(vals.reshape(depth, 1, n_gdn), ((0, 0), (0, 0), (8, 128 - 8 - n_gdn)))

    alog_rows = lane_row(gdn_a_log)
    dtb_rows = lane_row(gdn_dt_bias)

    cvecs = jnp.concatenate([c_ctx[None, :], c, jnp.zeros((nmod - 1 - bs, d), F32)], axis=0)
    mod = ada_modulation(cvecs, w_ada, b_ada)
    bias_tabs = attention_bias_tables(na_rpb, rows)
    past = cache_attn_kv.shape[4]
    kv_ctx = cache_attn_kv.transpose(0, 1, 2, 4, 3, 5).reshape(bs, depth, 2, past, NA_HEADS * NA_DIM)

    xp = x_prompt.reshape(bp * sp, d)
    xs = x_sample.reshape(bs * ss, d)
    kv_new, st_new = [], []
    hw = NA_HEADS * NA_DIM
    for l in range(depth):
        lw = {
            "g_pre_mix": g_pre_mix[l][None], "g_post_mix": g_post_mix[l][None],
            "g_pre_ffn": g_pre_ffn[l][None], "g_post_ffn": g_post_ffn[l][None],
            "w_head": w_head_b[l], "w_tail": w_tail_b[l], "w_out": w_out_b[l],
            "conv_w": conv_w[l], "conv_b": conv_b[l][None], "conv_ln_g": conv_ln_g[l][None],
            "conv_ln_b": conv_ln_b[l][None],
            "gdn_conv_w": gdn_conv_w[l], "alog_row": alog_rows[l], "dtb_row": dtb_rows[l],
            "gdn_norm_w": gdn_norm_w[l][None],
            "wg": wg_b[l], "wv": wv_b[l], "wd": wd_b[l],
            "cwg": ffn_conv_w[l][:, :D_FF], "cwv": ffn_conv_w[l][:, D_FF:],
            "cbg": ffn_conv_b[l][None, :D_FF], "cbv": ffn_conv_b[l][None, D_FF:],
        }
        xp, p3, s_fin = _trunk_layer(xp, lw, l, mod, batch=bp, seq=sp, row0=0, per_seq=bp * sp, ctx=None)
        heads = lambda col: p3[:, :, col:col + hw].reshape(bp, sp, NA_HEADS, NA_DIM).transpose(0, 2, 1, 3)
        kv_new.append(jnp.stack([heads(REST_KC), heads(REST_VC)], axis=1))
        st_new.append(s_fin)
        xs, _, _ = _trunk_layer(xs, lw, l, mod, batch=bs, seq=ss, row0=1, per_seq=ss,
                                ctx=(kv_ctx, bias_tabs, state_delta))
    return (xp.reshape(bp, sp, d), xs.reshape(bs, ss, d), jnp.stack(kv_new, axis=1), jnp.stack(st_new, axis=1))
```

```python
import functools
import math

import jax
import jax.numpy as jnp
from jax import lax
from jax.experimental import pallas as pl
from jax.experimental.pallas import tpu as pltpu

F32 = jnp.float32
BF16 = jnp.bfloat16
EPS = 1e-6
NEG = -1e30

D_MODEL = 1024
CONV_CH = 256
CONV_K = 31
GDN_HEADS = 4
GDN_D = 128
NA_HEADS = 4
NA_DIM = 64
GRID_W = 64
WIN_H = 8
WIN_W = 16
D_FF = 2816
CHUNK = 64
GDN_BLOCK = 1024
GDN_SUB = 256
NA_QROWS = 4
NA_WROWS = 12

COL_AV, COL_AG = 0, 256
COL_Q, COL_K, COL_V, COL_Z = 512, 1024, 1536, 2048
COL_QC, COL_KC, COL_VC = 2560, 2816, 3072
COL_BD = 3328
P_DIM = 3456

VMEM_LIMIT = 56 * 1024 * 1024


def _cparams(sem, vmem=None):
    return pltpu.CompilerParams(dimension_semantics=sem, vmem_limit_bytes=vmem)


def _const_spec(shape, index_map):
    return pl.BlockSpec(shape, index_map, pipeline_mode=pl.Buffered(1))


def _silu(x):
    return x * jax.nn.sigmoid(x)


def _rms(x, g):
    return x * lax.rsqrt(jnp.mean(x * x, axis=-1, keepdims=True) + EPS) * g


def _dot(a, b):
    return jnp.dot(a, b, preferred_element_type=F32)


def _dot_nt(a, b):
    return lax.dot_general(a, b, (((1,), (1,)), ((), ())), preferred_element_type=F32)


def _dot_tn(a, b):
    return lax.dot_general(a, b, (((0,), (0,)), ((), ())), preferred_element_type=F32)


def _mm(a, b):
    return _dot(a.astype(BF16), b.astype(BF16))


def _half_rows(x, s, half):
    n, c = x.shape[0] // (2 * s), x.shape[1]
    return x.reshape(n, 2, s, c)[:, half].reshape(n * s, c)


def _expand_rows(y, s, half):
    n, c = y.shape[0] // s, y.shape[1]
    y4 = y.reshape(n, 1, s, c)
    z = jnp.zeros_like(y4)
    return jnp.concatenate([z, y4] if half == 1 else [y4, z], axis=1).reshape(2 * n * s, c)


def _ada_kernel(c_ref, w_ref, b_ref, o_ref):
    s = _silu(c_ref[...]).astype(BF16)
    o_ref[0, 0] = _dot(s, w_ref[0].astype(BF16)) + b_ref[0, 0]


def ada_modulation(cvecs, w_ada, b_ada):
    depth = w_ada.shape[0]
    n = cvecs.shape[0]
    b4 = b_ada.reshape(depth, 6, 1, D_MODEL)
    return pl.pallas_call(
        _ada_kernel,
        out_shape=jax.ShapeDtypeStruct((depth, 6, n, D_MODEL), F32),
        grid=(depth, 6),
        in_specs=[pl.BlockSpec((n, D_MODEL), lambda l, j: (0, 0)),
                  pl.BlockSpec((1, D_MODEL, D_MODEL), lambda l, j: (l, 0, j)),
                  pl.BlockSpec((1, 1, 1, D_MODEL), lambda l, j: (l, j, 0, 0))],
        out_specs=pl.BlockSpec((1, 1, n, D_MODEL), lambda l, j: (l, j, 0, 0)),
        compiler_params=_cparams(("arbitrary", "arbitrary")),
        name="ada_modulation",
    )(cvecs, w_ada, b4)


MIX_HALO = 16
CONF_ROWS = 64
BG_ROWS = 24
REST_COLS = COL_BD - COL_Z
REST_Z, REST_QC, REST_KC, REST_VC = 0, COL_QC - COL_Z, COL_KC - COL_Z, COL_VC - COL_Z


def _mix_in_kernel(x_ref, xp_ref, xn_ref, mod_ref, g_ref, wh_ref, wt_ref, cw_ref, cb_ref, lg_ref, lb_ref, gcw_ref,
                   alog_ref, dtb_ref, rest_ref, oa_ref, qo_ref, ko_ref, vo_ref, bg_ref, bgt_ref,
                   hext_scr, u_scr, rot_scr, *, row0, tiles_per_mod, tiles_per_seq, tm):
    i = pl.program_id(0)
    row = row0 + i // tiles_per_mod
    t_in_seq = i % tiles_per_seq
    keep_prev = (t_in_seq > 0).astype(F32)
    keep_next = (t_in_seq < tiles_per_seq - 1).astype(F32)
    sh = mod_ref[0, 0, pl.ds(row, 1), :]
    sc = mod_ref[0, 1, pl.ds(row, 1), :]
    hh = MIX_HALO
    gw = GDN_HEADS * GDN_D

    def norm(x):
        return (_rms(x, g_ref[...]) * (1.0 + sc) + sh).astype(BF16)

    hext_scr[0:hh, :] = norm(xp_ref[...])
    hext_scr[hh:hh + tm, :] = norm(x_ref[...])
    hext_scr[hh + tm:2 * hh + tm, :] = norm(xn_ref[...])

    def proj_ext(col, width):
        return _dot(hext_scr[...], wh_ref[0, :, col:col + width])

    def proj(col, width):
        w_ref, col = (wh_ref, col) if col < COL_QC else (wt_ref, col - COL_QC)
        return _dot(hext_scr[hh:hh + tm, :], w_ref[0, :, col:col + width])

    pa = proj_ext(COL_AV, 2 * CONV_CH)
    pq = proj_ext(COL_Q, gw)
    u = pa[:, :CONV_CH] * jax.nn.sigmoid(pa[:, CONV_CH:])
    u_scr[0:hh, :] = u[0:hh] * keep_prev
    u_scr[hh:hh + tm, :] = u[hh:hh + tm]
    u_scr[hh + tm:2 * hh + tm, :] = u[hh + tm:] * keep_next
    n_ext = tm + 2 * hh
    u_ext = u_scr[...]
    for r in range(1, 8):
        rot_scr[r - 1] = pltpu.roll(u_ext, n_ext - r, 0)
    pk = proj_ext(COL_K, gw)
    half = CONV_K // 2
    for r0 in range(0, tm, CONF_ROWS):
        acc = jnp.zeros((CONF_ROWS, CONV_CH), F32) + cb_ref[...]
        for k in range(CONV_K):
            off = r0 + k + hh - half
            r = off % 8
            src = u_scr[off:off + CONF_ROWS, :] if r == 0 else rot_scr[r - 1, off - r:off - r + CONF_ROWS, :]
            acc = acc + cw_ref[pl.ds(k, 1), :] * src
        mu = jnp.mean(acc, axis=-1, keepdims=True)
        cen = acc - mu
        var = jnp.mean(cen * cen, axis=-1, keepdims=True)
        y = cen * lax.rsqrt(var + EPS) * lg_ref[...] + lb_ref[...]
        oa_ref[r0:r0 + CONF_ROWS, :] = _silu(y).astype(BF16)

    row8 = lax.broadcasted_iota(jnp.int32, (8, gw), 0)

    def conv_silu(pe, j):
        x = pe[hh:hh + tm]
        prev_row = pe[hh - 1:hh] * keep_prev
        next_row = pe[hh + tm:hh + tm + 1] * keep_next
        up = pltpu.roll(x, 1, 0)
        dn = pltpu.roll(x, tm - 1, 0)
        up = jnp.concatenate([jnp.where(row8 == 0, prev_row, up[0:8]), up[8:]], axis=0)
        dn = jnp.concatenate([dn[:tm - 8], jnp.where(row8 == 7, next_row, dn[tm - 8:])], axis=0)
        w = gcw_ref[:, j * gw:(j + 1) * gw]
        return _silu(up * w[0:1, :] + x * w[1:2, :] + dn * w[2:3, :])

    def l2n(seg, scale):
        return seg * (lax.rsqrt(jnp.sum(seg * seg, axis=-1, keepdims=True) + EPS) * scale)

    pv = proj_ext(COL_V, gw)
    q = conv_silu(pq, 0)
    for h in range(GDN_HEADS):
        sl = slice(h * GDN_D, (h + 1) * GDN_D)
        qo_ref[:, sl] = l2n(q[:, sl], GDN_D ** -0.5)
    rest_ref[:, 0:gw] = proj(COL_Z, gw)
    k = conv_silu(pk, 1)
    for h in range(GDN_HEADS):
        sl = slice(h * GDN_D, (h + 1) * GDN_D)
        ko_ref[:, sl] = l2n(k[:, sl], 1.0)
    rest_ref[:, gw:2 * gw] = proj(COL_QC, gw)
    vo_ref[...] = conv_silu(pv, 2)
    tail = proj(COL_VC, P_DIM - COL_VC)
    rest_ref[:, 2 * gw:REST_COLS] = tail[:, :REST_COLS - 2 * gw]

    bd = tail[:, REST_COLS - 2 * gw:]
    lane = lax.broadcasted_iota(jnp.int32, (tm, 128), 1)
    rc = lax.broadcasted_iota(jnp.int32, (tm, 128), 0) & (CHUNK - 1)
    is_g = (lane >= 8) & (lane < 16)
    beta = jax.nn.sigmoid(bd)
    xg = bd + dtb_ref[...]
    softplus = jnp.maximum(xg, 0.0) + jnp.log1p(jnp.exp(-jnp.abs(xg)))
    g = jnp.where(is_g, -jnp.exp(alog_ref[...]) * softplus, 0.0)
    pre = g
    suf = g
    s = 1
    while s < CHUNK:
        pre = pre + jnp.where(rc >= s, pltpu.roll(pre, s, 0), 0.0)
        suf = suf + jnp.where(rc < CHUNK - s, pltpu.roll(suf, tm - s, 0), 0.0)
        s *= 2
    tot = pre + suf - g
    bg = jnp.where(lane < 8, beta,
                   jnp.where(lane < 12, pre,
                             jnp.where(lane < 16, suf, pltpu.roll(tot, 8, 1))))
    bg = jnp.where(lane < 24, bg, 0.0)
    bg_ref[...] = bg
    bgt_ref[0] = jnp.transpose(bg)[0:BG_ROWS, :]


def mix_in(x, mod, layer, g_pre, w_head, w_tail, conv_w, conv_b, ln_g, ln_b, gdn_conv_w, alog_row, dtb_row, *,
           row0, per_seq, seq_len, tm):
    t = x.shape[0]
    assert seq_len % tm == 0 and per_seq % seq_len == 0 and tm % CONF_ROWS == 0 and tm % CHUNK == 0
    gw = GDN_HEADS * GDN_D
    hb = tm // MIX_HALO
    last = t // MIX_HALO - 1
    tps = seq_len // tm
    kern = functools.partial(_mix_in_kernel, row0=row0, tiles_per_mod=per_seq // tm, tiles_per_seq=tps, tm=tm)
    rowblk = lambda n: pl.BlockSpec((tm, n), lambda i: (i, 0))
    const = lambda shape: _const_spec(shape, lambda i: (0,) * len(shape))
    seq = jax.ShapeDtypeStruct((t, gw), F32)
    return pl.pallas_call(
        kern,
        out_shape=(jax.ShapeDtypeStruct((t, REST_COLS), F32), jax.ShapeDtypeStruct((t, CONV_CH), BF16),
                   seq, seq, seq, jax.ShapeDtypeStruct((t, 128), F32),
                   jax.ShapeDtypeStruct((t // seq_len, BG_ROWS, seq_len), F32)),
        grid=(t // tm,),
        in_specs=[rowblk(D_MODEL),
                  pl.BlockSpec((MIX_HALO, D_MODEL), lambda i: (jnp.maximum(i * hb - 1, 0), 0)),
                  pl.BlockSpec((MIX_HALO, D_MODEL), lambda i: (jnp.minimum((i + 1) * hb, last), 0)),
                  _const_spec((1, 6, mod.shape[2], D_MODEL), lambda i: (layer, 0, 0, 0)),
                  const((1, D_MODEL)),
                  _const_spec((1, D_MODEL, COL_QC), lambda i: (layer, 0, 0)),
                  _const_spec((1, D_MODEL, P_DIM - COL_QC), lambda i: (layer, 0, 0)),
                  const((CONV_K, CONV_CH)), const((1, CONV_CH)), const((1, CONV_CH)), const((1, CONV_CH)),
                  const((3, 3 * gw)), const((1, 128)), const((1, 128))],
        out_specs=(rowblk(REST_COLS), rowblk(CONV_CH), rowblk(gw), rowblk(gw), rowblk(gw), rowblk(128),
                   pl.BlockSpec((1, BG_ROWS, tm), lambda i: (i // tps, 0, i % tps))),
        scratch_shapes=[pltpu.VMEM((tm + 2 * MIX_HALO, D_MODEL), BF16),
                        pltpu.VMEM((tm + 2 * MIX_HALO, CONV_CH), F32),
                        pltpu.VMEM((7, tm + 2 * MIX_HALO, CONV_CH), F32)],
        compiler_params=_cparams(("parallel",), VMEM_LIMIT),
        name="mix_in",
    )(x, x, x, mod, g_pre, w_head, w_tail, conv_w, conv_b, ln_g, ln_b, gdn_conv_w, alog_row, dtb_row)


def _gscan_kernel(qf_ref, kf_ref, vf_ref, bgf_ref, bgtf_ref, qb_ref, kb_ref, vb_ref, bgb_ref, bgtb_ref,
                  s0_ref, of_ref, ob_ref, sfin_ref, s_scr, vn_scr, *, r):
    nb = pl.program_id(1)
    sub = min(GDN_SUB, r)
    nsub = r // sub
    nchunk = sub // CHUNK
    shift = int(math.log2(CHUNK))

    @pl.when(nb == 0)
    def _():
        s_scr[...] = s0_ref[0, 0]

    vn_scr[...] = jnp.zeros((2 * GDN_HEADS, sub, GDN_D), F32)
    ii = lax.broadcasted_iota(jnp.int32, (sub, sub), 0)
    jj = lax.broadcasted_iota(jnp.int32, (sub, sub), 1)
    blk_same = {s: (ii >> s) == (jj >> s) for s in range(3, shift + 1)}
    same = blk_same[shift]
    eye = (ii == jj).astype(F32)
    dirs = ((qf_ref, kf_ref, vf_ref, bgf_ref, bgtf_ref, of_ref), (qb_ref, kb_ref, vb_ref, bgb_ref, bgtb_ref, ob_ref))
    masks = ((same & (ii >= jj), same & (ii > jj)), (same & (ii <= jj), same & (ii < jj)))
    sl_h = [slice(h * GDN_D, (h + 1) * GDN_D) for h in range(GDN_HEADS)]
    nblk = 2 * GDN_HEADS

    def prepare(group, out):
        rs = [slice(sb * sub, (sb + 1) * sub) for sb, d, h in group]

        def each(fn):
            return [fn(i, group[i][1], group[i][2]) for i in range(nblk)]

        def col(off):
            return each(lambda i, d, h: dirs[d][3][0, rs[i], off + d * GDN_HEADS + h:off + d * GDN_HEADS + h + 1])

        beta, gcc, gl = col(0), col(8), col(16)
        gcr = each(lambda i, d, h: dirs[d][4][0, 8 + d * GDN_HEADS + h:9 + d * GDN_HEADS + h, rs[i]])
        q = each(lambda i, d, h: dirs[d][0][0, rs[i], sl_h[h]])
        k = each(lambda i, d, h: dirs[d][1][0, rs[i], sl_h[h]])
        v = each(lambda i, d, h: dirs[d][2][0, rs[i], sl_h[h]])
        decay = each(lambda i, d, h: jnp.exp(jnp.where(masks[d][0], gcc[i] - gcr[i], NEG)))
        kbeta = each(lambda i, d, h: k[i] * beta[i])
        kb16 = each(lambda i, d, h: k[i].astype(BF16))
        yield
        lm = each(lambda i, d, h: jnp.where(masks[d][1], _dot_nt(kbeta[i].astype(BF16), kb16[i]) * decay[i], 0.0))
        egc = each(lambda i, d, h: jnp.exp(gcc[i]))
        x = each(lambda i, d, h: jnp.concatenate([v[i] * beta[i], kbeta[i] * egc[i]], axis=1))
        yield
        l8 = each(lambda i, d, h: jnp.where(blk_same[3], lm[i], 0.0))
        m2 = each(lambda i, d, h: _mm(l8[i], l8[i]))
        yield
        m4 = each(lambda i, d, h: _mm(m2[i], m2[i]))
        t = each(lambda i, d, h: eye - l8[i])
        t = each(lambda i, d, h: t[i] + _mm(t[i], m2[i]))
        yield
        t = each(lambda i, d, h: t[i] + _mm(t[i], m4[i]))
        yield
        for lvl in range(3, shift):
            s_rows = 1 << lvl
            off = blk_same[lvl + 1] & jnp.logical_not(blk_same[lvl])
            ct = each(lambda i, d, h: _mm(_half_rows(jnp.where(off, lm[i], 0.0), s_rows, 1 - d), t[i]))
            yield
            upd = each(lambda i, d, h: _mm(_half_rows(t[i], s_rows, 1 - d), _expand_rows(ct[i], s_rows, 1 - d)))
            t = each(lambda i, d, h: t[i] - _expand_rows(upd[i], s_rows, 1 - d))
            yield
        out["x"] = each(lambda i, d, h: _mm(t[i], x[i]))
        yield
        out["intra"] = each(lambda i, d, h: (_dot_nt(q[i].astype(BF16), kb16[i]) * decay[i]).astype(BF16))
        out["qd"] = each(lambda i, d, h: q[i] * egc[i])
        out["kd"] = each(lambda i, d, h: (k[i] * jnp.exp(gl[i] - gcc[i])).astype(BF16))
        out["eg_last"] = each(lambda i, d, h: jnp.exp(jnp.broadcast_to(gl[i], (sub, GDN_D))))
        yield

    def scan(group, pre, state):
        x, qd, kd, intra, eg_last = pre["x"], pre["qd"], pre["kd"], pre["intra"], pre["eg_last"]
        outs = [[None] * nchunk for _ in range(nblk)]
        for step in range(nchunk):
            cs = [step if d == 0 else nchunk - 1 - step for sb, d, h in group]
            sls = [slice(c * CHUNK, (c + 1) * CHUNK) for c in cs]
            a = [_dot(jnp.concatenate([x[i][sls[i], GDN_D:], qd[i][sls[i]]], axis=0).astype(BF16),
                      state[group[i][1:]].astype(BF16)) for i in range(nblk)]
            v_new = [x[i][sls[i], :GDN_D] - a[i][:CHUNK] for i in range(nblk)]
            for i in range(nblk):
                vn_scr[i, sls[i], :] = v_new[i]
            yield
            for i in range(nblk):
                outs[i][cs[i]] = a[i][CHUNK:] + _dot(intra[i][sls[i], :], vn_scr[i].astype(BF16))
            for i in range(nblk):
                key = group[i][1:]
                state[key] = (state[key] * eg_last[i][cs[i] * CHUNK:cs[i] * CHUNK + 1, :]
                              + _dot_tn(kd[i][sls[i]], v_new[i].astype(BF16)))
            yield
        for i, (sb, d, h) in enumerate(group):
            dirs[d][5][0, sb * sub:(sb + 1) * sub, sl_h[h]] = jnp.concatenate(outs[i], axis=0)

    def run(*gens):
        gens = [g for g in gens if g is not None]
        while gens:
            for g in list(gens):
                try:
                    next(g)
                except StopIteration:
                    gens.remove(g)

    groups = [[(p, 0, h) for h in range(GDN_HEADS)] + [(nsub - 1 - p, 1, h) for h in range(GDN_HEADS)]
              for p in range(nsub)]
    state = {(d, h): s_scr[d, h] for d in range(2) for h in range(GDN_HEADS)}
    prev = None
    for p in range(nsub):
        pre = {}
        run(prepare(groups[p], pre), scan(groups[p - 1], prev, state) if p else None)
        prev = pre
    run(scan(groups[nsub - 1], prev, state))
    for (d, h), val in state.items():
        s_scr[d, h] = val
    sfin_ref[0] = s_scr[...]


def gdn_scan(q, k, v, bg, bgt, state0, layer):
    b, l, gw = q.shape
    r = min(GDN_BLOCK, l)
    nbk = l // r
    kern = functools.partial(_gscan_kernel, r=r)
    fwd = lambda i, n: (i, n, 0)
    bwd = lambda i, n: (i, nbk - 1 - n, 0)
    fwd_t = lambda i, n: (i, 0, n)
    bwd_t = lambda i, n: (i, 0, nbk - 1 - n)
    seq = lambda im: pl.BlockSpec((1, r, gw), im)
    st_shape = (2, GDN_HEADS, GDN_D, GDN_D)
    return pl.pallas_call(
        kern,
        out_shape=(jax.ShapeDtypeStruct((b, l, gw), F32), jax.ShapeDtypeStruct((b, l, gw), F32),
                   jax.ShapeDtypeStruct((b,) + st_shape, F32)),
        grid=(b, nbk),
        in_specs=[seq(fwd), seq(fwd), seq(fwd), pl.BlockSpec((1, r, 128), fwd), pl.BlockSpec((1, BG_ROWS, r), fwd_t),
                  seq(bwd), seq(bwd), seq(bwd), pl.BlockSpec((1, r, 128), bwd), pl.BlockSpec((1, BG_ROWS, r), bwd_t),
                  pl.BlockSpec((1, 1) + st_shape, lambda i, n: (i, layer, 0, 0, 0, 0))],
        out_specs=(seq(fwd), seq(bwd), pl.BlockSpec((1,) + st_shape, lambda i, n: (i, 0, 0, 0, 0))),
        scratch_shapes=[pltpu.VMEM(st_shape, F32), pltpu.VMEM((2 * GDN_HEADS, min(GDN_SUB, r), GDN_D), F32)],
        compiler_params=_cparams(("parallel", "arbitrary"), VMEM_LIMIT),
        name="gdn_scan",
    )(q, k, v, bg, bgt, q, k, v, bg, bgt, state0)


def _softmax_pv(q, parts):
    heads = range(NA_HEADS)
    lane = lax.broadcasted_iota(jnp.int32, (1, NA_HEADS * NA_DIM), 1)
    hm = [(lane >= h * NA_DIM) & (lane < (h + 1) * NA_DIM) for h in heads]
    qh = [jnp.where(hm[h], q, 0.0).astype(BF16) for h in heads]
    s = [[_dot_nt(qh[h], keys) if bias is None else _dot_nt(qh[h], keys) + bias(h) for keys, _, bias in parts]
         for h in heads]
    mx = [functools.reduce(jnp.maximum, [jnp.max(sp, axis=-1, keepdims=True) for sp in s[h]]) for h in heads]
    e = [[jnp.exp(sp - mx[h]) for sp in s[h]] for h in heads]
    inv = [1.0 / sum(jnp.sum(ep, axis=-1, keepdims=True) for ep in e[h]) for h in heads]
    o = [sum(_dot(ep.astype(BF16), vals) for ep, (_, vals, _) in zip(e[h], parts)) * inv[h] for h in heads]
    out = o[NA_HEADS - 1]
    for h in range(NA_HEADS - 2, -1, -1):
        out = jnp.where(hm[h], o[h], out)
    return out


def _attn_ctx_kernel(q_ref, k_ref, v_ref, o_ref):
    q = q_ref[0] * (NA_DIM ** -0.5)
    o_ref[0] = _softmax_pv(q, [(k_ref[0].astype(BF16), v_ref[0].astype(BF16), None)]).astype(BF16)


def context_attention(p3):
    b, l, _ = p3.shape
    w = NA_HEADS * NA_DIM
    col = lambda c: pl.BlockSpec((1, l, w), lambda i: (i, 0, c // w))
    return pl.pallas_call(
        _attn_ctx_kernel,
        out_shape=jax.ShapeDtypeStruct((b, l, w), BF16),
        grid=(b,),
        in_specs=[col(REST_QC), col(REST_KC), col(REST_VC)],
        out_specs=pl.BlockSpec((1, l, w), lambda i: (i, 0, 0)),
        compiler_params=_cparams(("parallel",)),
        name="context_attention",
    )(p3, p3, p3)


def _attn_lat_kernel(q_ref, k_ref, v_ref, kc_ref, vc_ref, bias_ref, o_ref, *, rows):
    i = pl.program_id(1)
    nwin = NA_WROWS * GRID_W
    ws = jnp.clip(i * NA_QROWS - WIN_H // 2, 0, rows - NA_WROWS)
    start = pl.multiple_of(ws * GRID_W, GRID_W)
    q = q_ref[0] * (NA_DIM ** -0.5)
    window = (k_ref[0, pl.ds(start, nwin), :].astype(BF16), v_ref[0, pl.ds(start, nwin), :].astype(BF16),
              lambda h: bias_ref[0, 0, h])
    context = (kc_ref[0, 0, 0].astype(BF16), vc_ref[0, 0, 0].astype(BF16), None)
    o_ref[0] = _softmax_pv(q, [window, context]).astype(BF16)


def neighbourhood_attention(p3, kv_ctx, bias_tabs, layer):
    b, n, _ = p3.shape
    rows = n // GRID_W
    w = NA_HEADS * NA_DIM
    tq = NA_QROWS * GRID_W
    ni = rows // NA_QROWS
    nctx = kv_ctx.shape[3]
    kern = functools.partial(_attn_lat_kernel, rows=rows)
    case = lambda i, t: (layer, jnp.where(t == 0, 0, jnp.where(t == ni - 1, 2, 1)), 0, 0, 0)
    return pl.pallas_call(
        kern,
        out_shape=jax.ShapeDtypeStruct((b, n, w), BF16),
        grid=(b, ni),
        in_specs=[pl.BlockSpec((1, tq, w), lambda i, t: (i, t, REST_QC // w)),
                  pl.BlockSpec((1, n, w), lambda i, t: (i, 0, REST_KC // w)),
                  pl.BlockSpec((1, n, w), lambda i, t: (i, 0, REST_VC // w)),
                  pl.BlockSpec((1, 1, 1, nctx, w), lambda i, t: (i, layer, 0, 0, 0)),
                  pl.BlockSpec((1, 1, 1, nctx, w), lambda i, t: (i, layer, 1, 0, 0)),
                  pl.BlockSpec((1, 1, NA_HEADS, tq, NA_WROWS * GRID_W), case)],
        out_specs=pl.BlockSpec((1, tq, w), lambda i, t: (i, t, 0)),
        compiler_params=_cparams(("parallel", "arbitrary"), VMEM_LIMIT),
        name="neighbourhood_attention",
    )(p3, p3, p3, kv_ctx, kv_ctx, bias_tabs)


def _bias_kernel(rpb_ref, o_ref, *, rows):
    l = pl.program_id(0)
    h = pl.program_id(1)
    n_dr, n_dc = 2 * WIN_H - 1, 2 * WIN_W - 1
    base = (l * NA_HEADS + h) * (n_dr * n_dc)
    c = lax.broadcasted_iota(jnp.int32, (GRID_W, 2 * GRID_W), 0)
    lane = lax.broadcasted_iota(jnp.int32, (GRID_W, 2 * GRID_W), 1)
    kc = lane & (GRID_W - 1)
    dc = jnp.clip(kc - c + (WIN_W - 1), 0, n_dc - 1)
    cs = jnp.clip(c - WIN_W // 2, 0, GRID_W - WIN_W)
    col_ok = (kc >= cs) & (kc < cs + WIN_W)
    neg = jnp.full((GRID_W, 2 * GRID_W), NEG, F32)
    tabs = []
    for a in range(n_dr):
        t = jnp.zeros((GRID_W, 2 * GRID_W), F32)
        for bb in range(n_dc):
            t = jnp.where(dc == bb, rpb_ref[base + a * n_dc + bb], t)
        tabs.append(jnp.where(col_ok, t, NEG))
    interior = 2 * NA_QROWS
    cases = ((0, 0), (interior, interior - WIN_H // 2), (rows - NA_QROWS, rows - NA_WROWS))
    for case, (r0, ws) in enumerate(cases):
        for iq in range(NA_QROWS):
            r = r0 + iq
            rs = min(max(r - WIN_H // 2, 0), rows - WIN_H)
            for jp in range(NA_WROWS // 2):
                halves = []
                for half in range(2):
                    kr = ws + 2 * jp + half
                    halves.append(tabs[kr - r + WIN_H - 1] if rs <= kr < rs + WIN_H else neg)
                blk = jnp.where(lane < GRID_W, halves[0], halves[1])
                o_ref[0, case, 0, iq * GRID_W:(iq + 1) * GRID_W, jp * 2 * GRID_W:(jp + 1) * 2 * GRID_W] = blk


def attention_bias_tables(na_rpb, rows):
    depth = na_rpb.shape[0]
    kern = functools.partial(_bias_kernel, rows=rows)
    tq, nw = NA_QROWS * GRID_W, NA_WROWS * GRID_W
    return pl.pallas_call(
        kern,
        out_shape=jax.ShapeDtypeStruct((depth, 3, NA_HEADS, tq, nw), F32),
        grid=(depth, NA_HEADS),
        in_specs=[pl.BlockSpec(memory_space=pltpu.SMEM)],
        out_specs=pl.BlockSpec((1, 3, 1, tq, nw), lambda l, h: (l, 0, h, 0, 0)),
        compiler_params=_cparams(("arbitrary", "arbitrary")),
        name="attention_bias_tables",
    )(na_rpb.reshape(-1))


FFN_HALO = 16
FFN_CHUNK = 256


def _out_ffn_kernel(x_ref, xp_ref, xn_ref, oa_ref, oap_ref, oan_ref, of_ref, ofp_ref, ofn_ref,
                    ob_ref, obp_ref, obn_ref, z_ref, zp_ref, zn_ref, oc_ref, ocp_ref, ocn_ref,
                    wo_ref, wu_ref, wd_ref, cwg_ref, cwv_ref, cbg_ref, cbv_ref,
                    mod_ref, gpm_ref, gpf_ref, gnw_ref, gpo_ref, o_ref,
                    mix_scr, hext_scr, act_scr, *, row0, tiles_per_mod, tiles_per_seq, tm):
    i = pl.program_id(0)
    row = row0 + i // tiles_per_mod
    g1 = mod_ref[0, 2, pl.ds(row, 1), :]
    sh2 = mod_ref[0, 3, pl.ds(row, 1), :]
    sc2 = mod_ref[0, 4, pl.ds(row, 1), :]
    g2 = mod_ref[0, 5, pl.ds(row, 1), :]
    hh = FFN_HALO
    gw = GDN_HEADS * GDN_D
    parts = ((0, tm, x_ref, oa_ref, of_ref, ob_ref, z_ref, oc_ref),
             (tm, hh, xp_ref, oap_ref, ofp_ref, obp_ref, zp_ref, ocp_ref),
             (tm + hh, hh, xn_ref, oan_ref, ofn_ref, obn_ref, zn_ref, ocn_ref))
    for r0, n, _, a_ref, f_ref, b_ref, zz_ref, c_ref in parts:
        mix_scr[r0:r0 + n, 0:CONV_CH] = a_ref[...]
        og = f_ref[...] + b_ref[...]
        zz = zz_ref[...]
        for h in range(GDN_HEADS):
            sl = slice(h * GDN_D, (h + 1) * GDN_D)
            gated = _rms(og[:, sl], gnw_ref[...]) * _silu(zz[:, sl])
            mix_scr[r0:r0 + n, CONV_CH + h * GDN_D:CONV_CH + (h + 1) * GDN_D] = gated.astype(BF16)
        mix_scr[r0:r0 + n, CONV_CH + gw:] = c_ref[...]
    m = _dot(mix_scr[...], wo_ref[0])
    x1 = None
    for r0, n, xx_ref, *_ in parts:
        x1_part = xx_ref[...] + g1 * _rms(m[r0:r0 + n], gpm_ref[...])
        hext_scr[r0:r0 + n, :] = (_rms(x1_part, gpf_ref[...]) * (1.0 + sc2) + sh2).astype(BF16)
        if r0 == 0:
            x1 = x1_part
    t_in_seq = i % tiles_per_seq
    keep_prev = (t_in_seq > 0).astype(F32)
    keep_next = (t_in_seq < tiles_per_seq - 1).astype(F32)
    row8 = lax.broadcasted_iota(jnp.int32, (8, FFN_CHUNK), 0)

    def conv(u_ext, cw_ref, cb_ref, sl):
        u = u_ext[0:tm]
        prev_row = u_ext[tm + hh - 1:tm + hh] * keep_prev
        next_row = u_ext[tm + hh:tm + hh + 1] * keep_next
        up = pltpu.roll(u, 1, 0)
        dn = pltpu.roll(u, tm - 1, 0)
        up = jnp.concatenate([jnp.where(row8 == 0, prev_row, up[0:8]), up[8:]], axis=0)
        dn = jnp.concatenate([dn[:tm - 8], jnp.where(row8 == 7, next_row, dn[tm - 8:])], axis=0)
        return up * cw_ref[0:1, sl] + u * cw_ref[1:2, sl] + dn * cw_ref[2:3, sl] + cb_ref[:, sl]

    n_chunks = D_FF // FFN_CHUNK
    sls = [slice(c * FFN_CHUNK, (c + 1) * FFN_CHUNK) for c in range(n_chunks)]

    def up_proj(c):
        val_cols = slice(D_FF + c * FFN_CHUNK, D_FF + (c + 1) * FFN_CHUNK)
        return _dot(hext_scr[...], wu_ref[0, :, sls[c]]), _dot(hext_scr[...], wu_ref[0, :, val_cols])

    nxt = up_proj(0)
    for c in range(n_chunks):
        ug, uv = nxt
        if c + 1 < n_chunks:
            nxt = up_proj(c + 1)
        gate = conv(ug, cwg_ref, cbg_ref, sls[c])
        val = conv(uv, cwv_ref, cbv_ref, sls[c])
        act_scr[:, sls[c]] = (_silu(gate) * val).astype(BF16)
    f = _dot(act_scr[...], wd_ref[0])
    o_ref[...] = x1 + g2 * _rms(f, gpo_ref[...])


def out_ffn(x, oa, of, ob, rest, oc, w_out, w_up, w_down, cwg, cwv, cbg, cbv, mod, layer, g_post_mix, g_pre_ffn, gnw,
            g_post_ffn, *, row0, per_seq, seq_len, tm):
    t = x.shape[0]
    assert seq_len % tm == 0 and per_seq % seq_len == 0
    gw = GDN_HEADS * GDN_D
    hb = tm // FFN_HALO
    last = t // FFN_HALO - 1
    kern = functools.partial(_out_ffn_kernel, row0=row0, tiles_per_mod=per_seq // tm, tiles_per_seq=seq_len // tm, tm=tm)

    def trio(n):
        return [pl.BlockSpec((tm, n), lambda i: (i, 0)),
                pl.BlockSpec((FFN_HALO, n), lambda i: (jnp.maximum(i * hb - 1, 0), 0)),
                pl.BlockSpec((FFN_HALO, n), lambda i: (jnp.minimum((i + 1) * hb, last), 0))]

    const = lambda shape: _const_spec(shape, lambda i: (0,) * len(shape))
    return pl.pallas_call(
        kern,
        out_shape=jax.ShapeDtypeStruct((t, D_MODEL), F32),
        grid=(t // tm,),
        in_specs=(trio(D_MODEL) + trio(CONV_CH) + trio(gw) + trio(gw) + trio(gw) + trio(NA_HEADS * NA_DIM)
                  + [_const_spec((1, D_MODEL, D_MODEL), lambda i: (layer, 0, 0)),
                     _const_spec((1, D_MODEL, 2 * D_FF), lambda i: (layer, 0, 0)),
                     _const_spec((1, D_FF, D_MODEL), lambda i: (layer, 0, 0)),
                     const((3, D_FF)), const((3, D_FF)), const((1, D_FF)), const((1, D_FF)),
                     _const_spec((1, 6, mod.shape[2], D_MODEL), lambda i: (layer, 0, 0, 0)),
                     const((1, D_MODEL)), const((1, D_MODEL)), const((1, GDN_D)), const((1, D_MODEL))]),
        out_specs=pl.BlockSpec((tm, D_MODEL), lambda i: (i, 0)),
        scratch_shapes=[pltpu.VMEM((tm + 2 * FFN_HALO, D_MODEL), BF16), pltpu.VMEM((tm + 2 * FFN_HALO, D_MODEL), BF16),
                        pltpu.VMEM((tm, D_FF), BF16)],
        compiler_params=_cparams(("parallel",), VMEM_LIMIT),
        name="out_ffn",
    )(x, x, x, oa, oa, oa, of, of, of, ob, ob, ob, rest, rest, rest, oc, oc, oc, w_out, w_up, w_down, cwg, cwv, cbg, cbv,
      mod, g_post_mix, g_pre_ffn, gnw, g_post_ffn)


def _trunk_layer(x, lw, layer, mod, *, batch, seq, row0, per_seq, ctx):
    tm = min(512, seq)
    rest, oa, qn, kn, vn, bg, bgt = mix_in(
        x, mod, layer, lw["g_pre_mix"], lw["w_head"], lw["w_tail"], lw["conv_w"], lw["conv_b"], lw["conv_ln_g"],
        lw["conv_ln_b"], lw["gdn_conv_w"], lw["alog_row"], lw["dtb_row"],
        row0=row0, per_seq=per_seq, seq_len=seq, tm=tm)
    gw = GDN_HEADS * GDN_D
    p3 = rest.reshape(batch, seq, REST_COLS)
    qn, kn, vn = (a.reshape(batch, seq, gw) for a in (qn, kn, vn))
    bg = bg.reshape(batch, seq, 128)
    if ctx is None:
        state0 = jnp.zeros((batch, 1, 2, GDN_HEADS, GDN_D, GDN_D), F32)
        of, ob, s_fin = gdn_scan(qn, kn, vn, bg, bgt, state0, 0)
        oc = context_attention(p3)
    else:
        kv_ctx, bias_tabs, state_delta = ctx
        of, ob, s_fin = gdn_scan(qn, kn, vn, bg, bgt, state_delta, layer)
        oc = neighbourhood_attention(p3, kv_ctx, bias_tabs, layer)
    t = batch * seq
    x2 = out_ffn(x, oa, of.reshape(t, gw), ob.reshape(t, gw), rest, oc.reshape(t, NA_HEADS * NA_DIM), lw["w_out"],
                 lw["w_up"], lw["w_down"], lw["cwg"], lw["cwv"], lw["cbg"], lw["cbv"], mod, layer,
                 lw["g_post_mix"], lw["g_pre_ffn"], lw["gdn_norm_w"], lw["g_post_ffn"],
                 row0=row0, per_seq=per_seq, seq_len=seq, tm=tm)
    return x2, p3, s_fin


def kernel(x_prompt, x_sample, cache_attn_kv, state_delta, c, c_ctx, w_ada, b_ada, g_pre_mix, g_post_mix, g_pre_ffn, g_post_ffn, w_in, w_out, conv_w, conv_b, conv_ln_g, conv_ln_b, gdn_conv_w, gdn_a_log, gdn_dt_bias, gdn_norm_w, na_rpb, w_up, ffn_conv_w, ffn_conv_b, w_down):
    depth = w_in.shape[0]
    bp, sp, d = x_prompt.shape
    bs, ss, _ = x_sample.shape
    rows = ss // GRID_W
    nmod = 16
    assert 1 + bs <= nmod

    n_gdn = 2 * GDN_HEADS
    c0 = 2 * CONV_CH + 4 * GDN_HEADS * GDN_D
    w_head_b = w_in[:, :, :c0].astype(BF16)
    w_tail_b = jnp.concatenate(
        [w_in[:, :, c0 + 2 * n_gdn:], w_in[:, :, c0:c0 + 2 * n_gdn],
         jnp.zeros((depth, d, P_DIM - w_in.shape[2]), w_in.dtype)], axis=-1).astype(BF16)
    w_out_b = w_out.astype(BF16)
    w_up_b = w_up.astype(BF16)
    w_down_b = w_down.astype(BF16)

    def lane_row(vals):
        return jnp.pad(vals.reshape(depth, 1, n_gdn), ((0, 0), (0, 0), (8, 128 - 8 - n_gdn)))

    alog_rows = lane_row(gdn_a_log)
    dtb_rows = lane_row(gdn_dt_bias)

    cvecs = jnp.concatenate([c_ctx[None, :], c, jnp.zeros((nmod - 1 - bs, d), F32)], axis=0)
    mod = ada_modulation(cvecs, w_ada, b_ada)
    bias_tabs = attention_bias_tables(na_rpb, rows)
    past = cache_attn_kv.shape[4]
    kv_ctx = cache_attn_kv.transpose(0, 1, 2, 4, 3, 5).reshape(bs, depth, 2, past, NA_HEADS * NA_DIM)

    xp = x_prompt.reshape(bp * sp, d)
    xs = x_sample.reshape(bs * ss, d)
    kv_new, st_new = [], []
    hw = NA_HEADS * NA_DIM
    for l in range(depth):
        lw = {
            "g_pre_mix": g_pre_mix[l][None], "g_post_mix": g_post_mix[l][None],
            "g_pre_ffn": g_pre_ffn[l][None], "g_post_ffn": g_post_ffn[l][None],
            "w_head": w_head_b, "w_tail": w_tail_b, "w_out": w_out_b, "w_up": w_up_b, "w_down": w_down_b,
            "conv_w": conv_w[l], "conv_b": conv_b[l][None], "conv_ln_g": conv_ln_g[l][None],
            "conv_ln_b": conv_ln_b[l][None],
            "gdn_conv_w": gdn_conv_w[l], "alog_row": alog_rows[l], "dtb_row": dtb_rows[l],
            "gdn_norm_w": gdn_norm_w[l][None],
            "cwg": ffn_conv_w[l][:, :D_FF], "cwv": ffn_conv_w[l][:, D_FF:],
            "cbg": ffn_conv_b[l][None, :D_FF], "cbv": ffn_conv_b[l][None, D_FF:],
        }
        xp, p3, s_fin = _trunk_layer(xp, lw, l, mod, batch=bp, seq=sp, row0=0, per_seq=bp * sp, ctx=None)
        heads = lambda col: p3[:, :, col:col + hw].reshape(bp, sp, NA_HEADS, NA_DIM).transpose(0, 2, 1, 3)
        kv_new.append(jnp.stack([heads(REST_KC), heads(REST_VC)], axis=1))
        st_new.append(s_fin)
        xs, _, _ = _trunk_layer(xs, lw, l, mod, batch=bs, seq=ss, row0=1, per_seq=ss,
                                ctx=(kv_ctx, bias_tabs, state_delta))
    return (xp.reshape(bp, sp, d), xs.reshape(bs, ss, d), jnp.stack(kv_new, axis=1), jnp.stack(st_new, axis=1))
```
